```python
import jax, jax.numpy as jnp
from jax import lax
import numpy as np

D_MODEL = 1024
BATCH = 32
SEQ = 2048
DEPTH = 2

GRID_W = 64
CTX_LEN = 256
HEAD_DIM = 64
ATTN_W = D_MODEL // 2
CONV_W = D_MODEL // 4
POOL_W = D_MODEL // 4
MIX_W = ATTN_W + CONV_W + POOL_W
ATTN_HEADS = ATTN_W // HEAD_DIM
KV_HEADS = ATTN_HEADS // 4
KV_W = KV_HEADS * HEAD_DIM
IN_W = ATTN_W + 2 * KV_W + 2 * CONV_W + POOL_W
WINDOW = 128
Q_BLOCK = 128
SPAN = Q_BLOCK + 2 * WINDOW
CONV_KERNEL = 31
POOL_WINDOWS = (2, 4, 8, 16)
POOL_GROUP = POOL_W // len(POOL_WINDOWS)
ROPE_BASE = 10000.0
D_FF = -(-8 * D_MODEL // (3 * 256)) * 256
EPS = 1e-6
NEG = -1e30

kernel_name = "hybrid_parallel_groups_dit_block"


def rms_norm(x, g):
    xf = x.astype(jnp.float32)
    y = xf * lax.rsqrt(jnp.mean(xf * xf, axis=-1, keepdims=True) + EPS)
    return (y * g.astype(jnp.float32)).astype(x.dtype)


def axial_rope_tables(n, dtype):
    rows = n // GRID_W
    row = jnp.repeat(jnp.arange(rows), GRID_W).astype(jnp.float32)
    col = jnp.tile(jnp.arange(GRID_W), rows).astype(jnp.float32)
    half = HEAD_DIM // 2
    inv = ROPE_BASE ** (-jnp.arange(0, half, 2, dtype=jnp.float32) / half)
    ar = row[:, None] * inv
    ac = col[:, None] * inv
    ang = jnp.concatenate([ar, ar, ac, ac], axis=-1)
    return jnp.cos(ang).astype(dtype), jnp.sin(ang).astype(dtype)


def apply_rope(x, cos, sin):
    xr = x.reshape(*x.shape[:-1], 2, 2, HEAD_DIM // 4)
    rot = jnp.stack([-xr[..., 1, :], xr[..., 0, :]], axis=-2).reshape(x.shape)
    return x * cos[:, None, :] + rot * sin[:, None, :]


def split_in(u):
    b, n, _ = u.shape
    q, k, v, cu, pu = jnp.split(
        u, [ATTN_W, ATTN_W + KV_W, ATTN_W + 2 * KV_W, ATTN_W + 2 * KV_W + 2 * CONV_W], axis=-1)
    return (q.reshape(b, n, ATTN_HEADS, HEAD_DIM), k.reshape(b, n, KV_HEADS, HEAD_DIM),
            v.reshape(b, n, KV_HEADS, HEAD_DIM), cu, pu)


def window_attention(q, k, v, k_ctx, v_ctx, sink):
    b, n, h, hd = q.shape
    kvh = k.shape[2]
    grp = h // kvh
    n_ctx = k_ctx.shape[1]
    nb = n // Q_BLOCK
    scale = HEAD_DIM ** -0.5
    pad = ((0, 0), (WINDOW, WINDOW), (0, 0), (0, 0))
    k_pad = jnp.pad(k, pad)
    v_pad = jnp.pad(v, pad)
    sink_b = jnp.broadcast_to(sink.astype(jnp.float32).reshape(1, kvh, grp, 1, 1), (b, kvh, grp, Q_BLOCK, 1))

    def one_block(i):
        start = i * Q_BLOCK
        qb = lax.dynamic_slice_in_dim(q, start, Q_BLOCK, axis=1).reshape(b, Q_BLOCK, kvh, grp, hd)
        kb = lax.dynamic_slice_in_dim(k_pad, start, SPAN, axis=1)
        vb = lax.dynamic_slice_in_dim(v_pad, start, SPAN, axis=1)
        qpos = start + jnp.arange(Q_BLOCK)
        kpos = start - WINDOW + jnp.arange(SPAN)
        valid = ((jnp.abs(qpos[:, None] - kpos[None, :]) <= WINDOW)
                 & (kpos >= 0)[None, :] & (kpos < n)[None, :])
        s_loc = jnp.einsum('bqkgd,bjkd->bkgqj', qb, kb).astype(jnp.float32) * scale
        s_loc = jnp.where(valid, s_loc, NEG)
        s_ctx = jnp.einsum('bqkgd,bckd->bkgqc', qb, k_ctx).astype(jnp.float32) * scale
        p = jax.nn.softmax(jnp.concatenate([sink_b, s_ctx, s_loc], axis=-1), axis=-1).astype(v.dtype)
        o = (jnp.einsum('bkgqc,bckd->bqkgd', p[..., 1:1 + n_ctx], v_ctx)
             + jnp.einsum('bkgqj,bjkd->bqkgd', p[..., 1 + n_ctx:], vb))
        return o.reshape(b, Q_BLOCK, h * hd)

    o = lax.map(one_block, jnp.arange(nb))
    return jnp.moveaxis(o, 0, 1).reshape(b, n, h * hd)


def context_attention(qc, kc, vc, sink):
    b, n_ctx, h, hd = qc.shape
    kvh = kc.shape[2]
    grp = h // kvh
    qg = qc.reshape(b, n_ctx, kvh, grp, hd)
    s = jnp.einsum('bqkgd,bckd->bkgqc', qg, kc).astype(jnp.float32) * (HEAD_DIM ** -0.5)
    sink_b = jnp.broadcast_to(sink.astype(jnp.float32).reshape(1, kvh, grp, 1, 1), (b, kvh, grp, n_ctx, 1))
    p = jax.nn.softmax(jnp.concatenate([sink_b, s], axis=-1), axis=-1).astype(vc.dtype)
    o = jnp.einsum('bkgqc,bckd->bqkgd', p[..., 1:], vc)
    return o.reshape(b, n_ctx, h * hd)


def conv_module(u, dw, dw_b, ln_g, ln_b):
    a, g = jnp.split(u, 2, axis=-1)
    h = a * jax.nn.sigmoid(g)
    h = lax.conv_general_dilated(
        h, dw[:, None, :], window_strides=(1,),
        padding=[(CONV_KERNEL // 2, CONV_KERNEL // 2)],
        dimension_numbers=('NWC', 'WIO', 'NWC'), feature_group_count=CONV_W) + dw_b
    hf = h.astype(jnp.float32)
    mu = jnp.mean(hf, axis=-1, keepdims=True)
    var = jnp.mean(jnp.square(hf - mu), axis=-1, keepdims=True)
    hn = (hf - mu) * lax.rsqrt(var + EPS) * ln_g.astype(jnp.float32) + ln_b.astype(jnp.float32)
    return jax.nn.silu(hn).astype(u.dtype)


def pool_mixer(p, w, scale):
    b, n, ch = p.shape
    t = jnp.arange(n)
    pf = p.astype(jnp.float32).reshape(b, n, len(POOL_WINDOWS), POOL_GROUP)
    cs = jnp.pad(jnp.cumsum(pf, axis=1), ((0, 0), (1, 0), (0, 0), (0, 0)))
    outs = []
    for gi, win in enumerate(POOL_WINDOWS):
        lo = jnp.maximum(t - win // 2, 0)
        hi = jnp.minimum(t + win - 1 - win // 2, n - 1)
        cg = cs[:, :, gi]
        mean = (cg[:, hi + 1] - cg[:, lo]) / (hi - lo + 1).astype(jnp.float32)[None, :, None]
        outs.append(mean - pf[:, :, gi])
    y = jnp.stack(outs, axis=2).astype(p.dtype)
    y = jnp.einsum('bsgc,gcd->bsgd', y, w).reshape(b, n, ch)
    return y * scale


def mixer_output(attn, cu, pu, w_out, dw, dw_b, ln_g, ln_b, pw, ps):
    conv = conv_module(cu, dw, dw_b, ln_g, ln_b)
    pool = pool_mixer(pu, pw, ps)
    return jnp.concatenate([attn, conv, pool], axis=-1) @ w_out


def swiglu(h, w_in, w_out):
    g, u = jnp.split(h @ w_in, 2, axis=-1)
    return (jax.nn.silu(g) * u) @ w_out


def setup_inputs(seed: int = 0) -> dict:
    key = jax.random.key(seed)
    ks = jax.random.split(key, 24)
    f32 = jnp.float32
    nrm = lambda k, shape, s: jax.random.normal(k, shape, f32) * s
    return {
        "x": nrm(ks[0], (BATCH, SEQ, D_MODEL), 1.0),
        "c": nrm(ks[1], (BATCH, D_MODEL), 1.0),
        "ctx": nrm(ks[2], (BATCH, CTX_LEN, D_MODEL), 1.0),
        "c_ctx": nrm(ks[3], (D_MODEL,), 1.0),
        "w_mod": nrm(ks[4], (DEPTH, D_MODEL, 6 * D_MODEL), 0.5 * D_MODEL ** -0.5),
        "b_mod": nrm(ks[5], (DEPTH, 6 * D_MODEL), 0.01),
        "norm1_g": 1.0 + nrm(ks[6], (DEPTH, D_MODEL), 0.05),
        "norm2_g": 1.0 + nrm(ks[7], (DEPTH, D_MODEL), 0.05),
        "w_in": nrm(ks[8], (DEPTH, D_MODEL, IN_W), D_MODEL ** -0.5),
        "conv_dw": nrm(ks[9], (DEPTH, CONV_KERNEL, CONV_W), CONV_KERNEL ** -0.5),
        "conv_dw_b": nrm(ks[10], (DEPTH, CONV_W), 0.01),
        "conv_ln_g": 1.0 + nrm(ks[11], (DEPTH, CONV_W), 0.05),
        "conv_ln_b": nrm(ks[12], (DEPTH, CONV_W), 0.01),
        "attn_sink": nrm(ks[13], (DEPTH, ATTN_HEADS), 0.5),
        "pool_w": nrm(ks[14], (DEPTH, len(POOL_WINDOWS), POOL_GROUP, POOL_GROUP), POOL_GROUP ** -0.5),
        "pool_scale": 1.0 + nrm(ks[15], (DEPTH, POOL_W), 0.05),
        "w_out": nrm(ks[16], (DEPTH, MIX_W, D_MODEL), MIX_W ** -0.5),
        "w_ffn_in": nrm(ks[17], (DEPTH, D_MODEL, 2 * D_FF), D_MODEL ** -0.5),
        "w_ffn_out": nrm(ks[18], (DEPTH, D_FF, D_MODEL), D_FF ** -0.5),
        "final_g": 1.0 + nrm(ks[19], (D_MODEL,), 0.05),
    }


def reference(x, c, ctx, c_ctx, w_mod, b_mod, norm1_g, norm2_g, w_in, conv_dw, conv_dw_b,
              conv_ln_g, conv_ln_b, attn_sink, pool_w, pool_scale, w_out, w_ffn_in, w_ffn_out, final_g):
    b, n, _ = x.shape
    n_ctx = ctx.shape[1]
    cos, sin = axial_rope_tables(n, x.dtype)
    cx = ctx
    for l in range(DEPTH):
        last = l == DEPTH - 1
        m = (jax.nn.silu(c) @ w_mod[l] + b_mod[l])[:, None, :]
        sh1, sc1, g1, sh2, sc2, g2 = jnp.split(m, 6, axis=-1)
        mc = jax.nn.silu(c_ctx) @ w_mod[l] + b_mod[l]
        csh1, csc1, cg1, csh2, csc2, cg2 = jnp.split(mc, 6)

        hl = rms_norm(x, norm1_g[l]) * (1.0 + sc1) + sh1
        hc = rms_norm(cx, norm1_g[l]) * (1.0 + csc1) + csh1
        q, k, v, cu, pu = split_in(hl @ w_in[l])
        if last:
            kvc = hc @ w_in[l][:, ATTN_W:ATTN_W + 2 * KV_W]
            kc, vc = [t.reshape(b, n_ctx, KV_HEADS, HEAD_DIM) for t in jnp.split(kvc, 2, axis=-1)]
        else:
            qc, kc, vc, cuc, puc = split_in(hc @ w_in[l])
        q = apply_rope(q, cos, sin)
        k = apply_rope(k, cos, sin)
        attn = window_attention(q, k, v, kc, vc, attn_sink[l])
        x = x + g1 * mixer_output(attn, cu, pu, w_out[l], conv_dw[l], conv_dw_b[l],
                                  conv_ln_g[l], conv_ln_b[l], pool_w[l], pool_scale[l])
        if not last:
            attn_c = context_attention(qc, kc, vc, attn_sink[l])
            cx = cx + cg1 * mixer_output(attn_c, cuc, puc, w_out[l], conv_dw[l], conv_dw_b[l],
                                         conv_ln_g[l], conv_ln_b[l], pool_w[l], pool_scale[l])

        x = x + g2 * swiglu(rms_norm(x, norm2_g[l]) * (1.0 + sc2) + sh2, w_ffn_in[l], w_ffn_out[l])
        if not last:
            cx = cx + cg2 * swiglu(rms_norm(cx, norm2_g[l]) * (1.0 + csc2) + csh2, w_ffn_in[l], w_ffn_out[l])
    return rms_norm(x, final_g)
```

```python
import functools

import jax
import jax.numpy as jnp
from jax import lax
from jax.experimental import pallas as pl
from jax.experimental.pallas import tpu as pltpu

F32 = jnp.float32
BF16 = jnp.bfloat16

LANES = 128
SUBLANES = 8
VMEM_LIMIT_BYTES = 56 * 1024 * 1024

GRID_W = 64
HEAD_DIM = 64
ROPE_BASE = 10000.0
WINDOW = 128
Q_BLOCK = 128
SPAN = Q_BLOCK + 2 * WINDOW
CONV_KERNEL = 31
CONV_PAD = CONV_KERNEL // 2
CONV_HALO = 16
POOL_WINDOWS = (2, 4, 8, 16)
POOL_HALO = 8
ROW_CHUNK = 64
EPS = 1e-6
NEG = -1e30


def _rms_norm(x, g):
    return x * lax.rsqrt(jnp.mean(x * x, axis=-1, keepdims=True) + EPS) * g


def _dot(a, b):
    return jnp.dot(a, b, preferred_element_type=F32)


def _dot_nt(a, b):
    return lax.dot_general(a, b, (((1,), (1,)), ((), ())), preferred_element_type=F32)


def _mod_kernel(c_ref, w_ref, b_ref, o_ref):
    c = c_ref[...]
    a = c * jax.nn.sigmoid(c)
    o_ref[0] = jnp.dot(a, w_ref[0], preferred_element_type=F32,
                       precision=lax.Precision.HIGHEST) + b_ref[0]


def _modulation(cc, w_mod, b_mod):
    depth, d, d6 = w_mod.shape
    rows = cc.shape[0]
    tn = d6 // 4
    return pl.pallas_call(
        _mod_kernel,
        grid=(depth, d6 // tn),
        in_specs=[
            pl.BlockSpec((rows, d), lambda l, j: (0, 0)),
            pl.BlockSpec((1, d, tn), lambda l, j: (l, 0, j)),
            pl.BlockSpec((1, 1, tn), lambda l, j: (l, 0, j)),
        ],
        out_specs=pl.BlockSpec((1, rows, tn), lambda l, j: (l, 0, j)),
        out_shape=jax.ShapeDtypeStruct((depth, rows, d6), F32),
        compiler_params=pltpu.CompilerParams(
            dimension_semantics=("arbitrary", "arbitrary"), vmem_limit_bytes=VMEM_LIMIT_BYTES),
        name="modulation",
    )(cc, w_mod, b_mod.reshape(depth, 1, d6))


def _rope(t, cos, sin_a, sin_b):
    quarter = HEAD_DIM // 4
    return (t * cos + pltpu.roll(t, LANES - quarter, 1) * sin_a + pltpu.roll(t, quarter, 1) * sin_b)


def _inproj_kernel(*refs, rope, kv_only, attn_w, kv_w, conv_w):
    if rope:
        x_ref, mod_ref, g_ref, w_ref, cos_ref, sa_ref, sb_ref = refs[:7]
        outs = refs[7:]
    else:
        x_ref, mod_ref, g_ref, w_ref = refs[:4]
        outs = refs[4:]
    x = x_ref[0]
    hl = _rms_norm(x, g_ref[...]) * (1.0 + mod_ref[0, 1:2, :]) + mod_ref[0, 0:1, :]
    u = _dot(hl.astype(BF16), w_ref[...])

    def maybe_rope(t):
        if rope:
            return _rope(t, cos_ref[...], sa_ref[...], sb_ref[...])
        return t

    if kv_only:
        kk_ref, vv_ref = outs
        off = 0
    else:
        q_ref, kk_ref, vv_ref, h_ref, p_ref = outs
        for c in range(attn_w // LANES):
            t = maybe_rope(u[:, c * LANES:(c + 1) * LANES])
            q_ref[0, :, c * LANES:(c + 1) * LANES] = (t * (HEAD_DIM ** -0.5)).astype(BF16)
        off = attn_w
    for c in range(kv_w // LANES):
        t = maybe_rope(u[:, off + c * LANES: off + (c + 1) * LANES])
        kk_ref[0, :, c * LANES:(c + 1) * LANES] = t.astype(BF16)
    off += kv_w
    vv_ref[0] = u[:, off:off + kv_w].astype(BF16)
    off += kv_w
    if not kv_only:
        a = u[:, off:off + conv_w]
        g = u[:, off + conv_w:off + 2 * conv_w]
        h_ref[0] = a * jax.nn.sigmoid(g)
        off += 2 * conv_w
        p_ref[0] = u[:, off:]


def _inproj(x, mod, g, w, tables, *, tm, kv_only, dims):
    b, n, d = x.shape
    attn_w, kv_w, conv_w, pool_w = dims
    rope = tables is not None
    grid = (n // tm, b)
    in_specs = [
        pl.BlockSpec((1, tm, d), lambda i, bb: (bb, i, 0)),
        pl.BlockSpec((1, 6, d), lambda i, bb: (bb, 0, 0)),
        pl.BlockSpec((1, d), lambda i, bb: (0, 0)),
        pl.BlockSpec(w.shape, lambda i, bb: (0, 0)),
    ]
    args = [x, mod, g, w]
    if rope:
        in_specs += [pl.BlockSpec((tm, LANES), lambda i, bb: (i, 0))] * 3
        args += list(tables)
    tok = lambda width: pl.BlockSpec((1, tm, width), lambda i, bb: (bb, i, 0))
    out_specs = [tok(kv_w), tok(kv_w)]
    out_shape = [jax.ShapeDtypeStruct((b, n, kv_w), BF16)] * 2
    if not kv_only:
        out_specs = [tok(attn_w)] + out_specs + [tok(conv_w), tok(pool_w)]
        out_shape = ([jax.ShapeDtypeStruct((b, n, attn_w), BF16)] + out_shape
                     + [jax.ShapeDtypeStruct((b, n, conv_w), F32), jax.ShapeDtypeStruct((b, n, pool_w), F32)])
    kern = functools.partial(_inproj_kernel, rope=rope, kv_only=kv_only,
                             attn_w=attn_w, kv_w=kv_w, conv_w=conv_w)
    return pl.pallas_call(
        kern, grid=grid, in_specs=in_specs, out_specs=out_specs, out_shape=out_shape,
        compiler_params=pltpu.CompilerParams(
            dimension_semantics=("arbitrary", "arbitrary"), vmem_limit_bytes=VMEM_LIMIT_BYTES),
        name="inproj_kv" if kv_only else "inproj",
    )(*args)


def _attention_block(q_ref, rows, kc, vc, kl, vl, lmask, sink_ref, mix_ref):
    lane = lax.broadcasted_iota(jnp.int32, (1, LANES), 1)
    low = lane < HEAD_DIM
    zero = jnp.zeros((), BF16)
    n_groups = kc.shape[1] // LANES
    heads_per_group = q_ref.shape[2] // HEAD_DIM // n_groups
    for g in range(n_groups):
        gs = slice(g * LANES, (g + 1) * LANES)
        halves = []
        for keep_low in (True, False):
            m = low if keep_low else jnp.logical_not(low)
            sel = lambda t: jnp.where(m, t, zero)
            halves.append((sel(kc[:, gs]), sel(vc[:, gs]),
                           None if kl is None else sel(kl[:, gs]),
                           None if vl is None else sel(vl[:, gs])))
        for pr in range(heads_per_group // 2):
            col = (g * heads_per_group // 2 + pr) * LANES
            q2 = q_ref[0, rows, col:col + LANES]
            out = None
            for hh, (kcm, vcm, klm, vlm) in enumerate(halves):
                sink = sink_ref[g * heads_per_group + 2 * pr + hh]
                s_c = _dot_nt(q2, kcm)
                mx = jnp.maximum(jnp.max(s_c, axis=-1, keepdims=True), sink)
                if klm is not None:
                    s_l = jnp.where(lmask, _dot_nt(q2, klm), NEG)
                    mx = jnp.maximum(mx, jnp.max(s_l, axis=-1, keepdims=True))
                e_c = jnp.exp(s_c - mx)
                den = jnp.exp(sink - mx) + jnp.sum(e_c, axis=-1, keepdims=True)
                o = _dot(e_c.astype(BF16), vcm)
                if klm is not None:
                    e_l = jnp.exp(s_l - mx)
                    den = den + jnp.sum(e_l, axis=-1, keepdims=True)
                    o = o + _dot(e_l.astype(BF16), vlm)
                o = o * (1.0 / den)
                out = o if out is None else out + o
            mix_ref[rows, col:col + LANES] = out.astype(BF16)


def _mixer_kernel(*refs, n, tm, has_local, final_norm, attn_w, conv_w, pool_w, d_ff):
    it = iter(refs)
    x_ref, q_ref = next(it), next(it)
    if has_local:
        kk_ref, vv_ref = next(it), next(it)
    kc_ref, vc_ref = next(it), next(it)
    h_ref, p_ref, mod_ref, sink_ref = next(it), next(it), next(it), next(it)
    dw_ref, cvec_ref, poolw_ref, wout_ref, g2_ref, wfi_ref, wfo_ref, gf_ref = (next(it) for _ in range(8))
    out_ref = next(it)
    mix_ref, hp_ref, pp_ref, y_ref = next(it), next(it), next(it), next(it)

    i = pl.program_id(1)
    last_i = pl.num_programs(1) - 1
    t0 = pl.multiple_of(i * tm, tm)

    kc = kc_ref[0]
    vc = vc_ref[0]
    for blk in range(tm // Q_BLOCK):
        rows = slice(blk * Q_BLOCK, (blk + 1) * Q_BLOCK)
        if has_local:
            start = t0 + blk * Q_BLOCK
            s0 = pl.multiple_of(jnp.clip(start - WINDOW, 0, n - SPAN), Q_BLOCK)
            r = lax.broadcasted_iota(jnp.int32, (Q_BLOCK, SPAN), 0)
            c = lax.broadcasted_iota(jnp.int32, (Q_BLOCK, SPAN), 1)
            dist = (start - s0) + r - c
            lmask = (dist >= -WINDOW) & (dist <= WINDOW)
            kl = kk_ref[0, pl.ds(s0, SPAN), :]
            vl = vv_ref[0, pl.ds(s0, SPAN), :]
        else:
            kl = vl = lmask = None
        _attention_block(q_ref, rows, kc, vc, kl, vl, lmask, sink_ref, mix_ref)

    hp_ref[CONV_HALO:CONV_HALO + tm, :] = h_ref[0, pl.ds(t0, tm), :]
    prev = h_ref[0, pl.ds(pl.multiple_of(jnp.maximum(t0 - CONV_HALO, 0), CONV_HALO), CONV_HALO), :]
    hp_ref[0:CONV_HALO, :] = jnp.where(i > 0, prev, 0.0)
    nxt = h_ref[0, pl.ds(pl.multiple_of(jnp.minimum(t0 + tm, n - CONV_HALO), CONV_HALO), CONV_HALO), :]
    hp_ref[CONV_HALO + tm:, :] = jnp.where(i < last_i, nxt, 0.0)
    dw_b, ln_g, ln_b, pool_scale = (cvec_ref[k:k + 1, :] for k in range(4))
    for r0 in range(0, tm, ROW_CHUNK):
        acc = jnp.zeros((ROW_CHUNK, conv_w), F32)
        for k in range(CONV_KERNEL):
            base = r0 + CONV_HALO - CONV_PAD + k
            acc = acc + hp_ref[base:base + ROW_CHUNK, :] * dw_ref[k:k + 1, :]
        hc = acc + dw_b
        mu = jnp.mean(hc, axis=-1, keepdims=True)
        cen = hc - mu
        var = jnp.mean(cen * cen, axis=-1, keepdims=True)
        hn = cen * lax.rsqrt(var + EPS) * ln_g + ln_b
        mix_ref[r0:r0 + ROW_CHUNK, attn_w:attn_w + conv_w] = (hn * jax.nn.sigmoid(hn)).astype(BF16)

    pp_ref[POOL_HALO:POOL_HALO + tm, :] = p_ref[0, pl.ds(t0, tm), :]
    prev = p_ref[0, pl.ds(pl.multiple_of(jnp.maximum(t0 - POOL_HALO, 0), POOL_HALO), POOL_HALO), :]
    pp_ref[0:POOL_HALO, :] = jnp.where(i > 0, prev, 0.0)
    nxt = p_ref[0, pl.ds(pl.multiple_of(jnp.minimum(t0 + tm, n - POOL_HALO), POOL_HALO), POOL_HALO), :]
    pp_ref[POOL_HALO + tm:, :] = jnp.where(i < last_i, nxt, 0.0)
    group_w = pool_w // len(POOL_WINDOWS)
    groups_per_col = LANES // group_w
    lane = lax.broadcasted_iota(jnp.int32, (ROW_CHUNK, LANES), 1)
    row = lax.broadcasted_iota(jnp.int32, (ROW_CHUNK, LANES), 0)
    for r0 in range(0, tm, ROW_CHUNK):
        t = t0 + r0 + row
        for col in range(pool_w // LANES):
            cs = slice(col * LANES, (col + 1) * LANES)
            wins = POOL_WINDOWS[col * groups_per_col:(col + 1) * groups_per_col]
            total = None
            cnt = None
            for gi, win in enumerate(wins):
                in_group = (lane >= gi * group_w) & (lane < (gi + 1) * group_w)
                back, fwd = win // 2, win - 1 - win // 2
                lo = jnp.maximum(t - back, 0)
                hi = jnp.minimum(t + fwd, n - 1)
                c_g = (hi - lo + 1).astype(F32)
                cnt = c_g if cnt is None else jnp.where(in_group, c_g, cnt)
            for off in range(-POOL_HALO, POOL_HALO):
                member = [(-(w // 2) <= off <= w - 1 - w // 2) for w in wins]
                if not any(member):
                    continue
                base = r0 + POOL_HALO + off
                val = pp_ref[base:base + ROW_CHUNK, cs]
                if not all(member):
                    keep = None
                    for gi, mem in enumerate(member):
                        if mem:
                            in_group = (lane >= gi * group_w) & (lane < (gi + 1) * group_w)
                            keep = in_group if keep is None else keep | in_group
                    val = jnp.where(keep, val, 0.0)
                total = val if total is None else total + val
            centre = pp_ref[r0 + POOL_HALO:r0 + POOL_HALO + ROW_CHUNK, cs]
            y_ref[r0:r0 + ROW_CHUNK, cs] = (total / cnt - centre).astype(BF16)
    pooled = _dot(y_ref[...], poolw_ref[...]) * pool_scale
    mix_ref[:, attn_w + conv_w:] = pooled.astype(BF16)

    x1 = x_ref[0] + mod_ref[0, 2:3, :] * _dot(mix_ref[...], wout_ref[...])
    y2 = _rms_norm(x1, g2_ref[...]) * (1.0 + mod_ref[0, 4:5, :]) + mod_ref[0, 3:4, :]
    gu = _dot(y2.astype(BF16), wfi_ref[...])
    gate = gu[:, :d_ff]
    act = (gate * jax.nn.sigmoid(gate) * gu[:, d_ff:]).astype(BF16)
    x2 = x1 + mod_ref[0, 5:6, :] * _dot(act, wfo_ref[...])
    if final_norm:
        x2 = _rms_norm(x2, gf_ref[...])
    out_ref[0] = x2


def _mixer(x, q, kk, vv, kc, vc, h, p, mod, sink, dw, cvec, poolw, wout, g2, wfi, wfo, gf,
           *, tm, final_norm):
    b, n, d = x.shape
    has_local = kk is not None
    attn_w, kv_w, conv_w, pool_w = q.shape[2], kc.shape[2], h.shape[2], p.shape[2]
    n_ctx = kc.shape[1]
    d_ff = wfo.shape[0]
    const = lambda shape: pl.BlockSpec(shape, lambda bb, i: (0,) * len(shape), pipeline_mode=pl.Buffered(1))
    tok = lambda width: pl.BlockSpec((1, tm, width), lambda bb, i: (bb, i, 0))
    seq = lambda length, width: pl.BlockSpec((1, length, width), lambda bb, i: (bb, 0, 0))
    in_specs = [tok(d), tok(attn_w)]
    args = [x, q]
    if has_local:
        in_specs += [seq(n, kv_w), seq(n, kv_w)]
        args += [kk, vv]
    in_specs += [seq(n_ctx, kv_w), seq(n_ctx, kv_w), seq(n, conv_w), seq(n, pool_w), seq(6, d),
                 pl.BlockSpec(memory_space=pltpu.SMEM),
                 const(dw.shape), const(cvec.shape), const(poolw.shape), const(wout.shape),
                 const(g2.shape), const(wfi.shape), const(wfo.shape), const(gf.shape)]
    args += [kc, vc, h, p, mod, sink, dw, cvec, poolw, wout, g2, wfi, wfo, gf]
    kern = functools.partial(_mixer_kernel, n=n, tm=tm, has_local=has_local, final_norm=final_norm,
                             attn_w=attn_w, conv_w=conv_w, pool_w=pool_w, d_ff=d_ff)
    return pl.pallas_call(
        kern, grid=(b, n // tm), in_specs=in_specs, out_specs=tok(d),
        out_shape=jax.ShapeDtypeStruct((b, n, d), F32),
        scratch_shapes=[
            pltpu.VMEM((tm, attn_w + conv_w + pool_w), BF16),
            pltpu.VMEM((tm + 2 * CONV_HALO, conv_w), F32),
            pltpu.VMEM((tm + 2 * POOL_HALO, pool_w), F32),
            pltpu.VMEM((tm, pool_w), BF16),
        ],
        compiler_params=pltpu.CompilerParams(
            dimension_semantics=("arbitrary", "arbitrary"), vmem_limit_bytes=VMEM_LIMIT_BYTES),
        name="mixer_local" if has_local else "mixer_ctx",
    )(*args)


def _rope_tables(n):
    rows = n // GRID_W
    row = jnp.repeat(jnp.arange(rows), GRID_W).astype(F32)
    col = jnp.tile(jnp.arange(GRID_W), rows).astype(F32)
    half = HEAD_DIM // 2
    inv = ROPE_BASE ** (-jnp.arange(0, half, 2, dtype=F32) / half)
    ar = row[:, None] * inv
    ac = col[:, None] * inv
    ang = jnp.concatenate([ar, ar, ac, ac], axis=-1)
    cos, sin = jnp.cos(ang), jnp.sin(ang)
    first = (jnp.arange(HEAD_DIM) // (HEAD_DIM // 4)) % 2 == 0
    sin_a = jnp.where(first, -sin, 0.0)
    sin_b = jnp.where(first, 0.0, sin)
    rep = LANES // HEAD_DIM
    return tuple(jnp.tile(t, (1, rep)) for t in (cos, sin_a, sin_b))


def _dup_heads(w):
    d, width = w.shape
    w = w.reshape(d, width // HEAD_DIM, 1, HEAD_DIM)
    return jnp.broadcast_to(w, (d, width // HEAD_DIM, 2, HEAD_DIM)).reshape(d, 2 * width)


def kernel(x, c, ctx, c_ctx, w_mod, b_mod, norm1_g, norm2_g, w_in, conv_dw, conv_dw_b, conv_ln_g, conv_ln_b,
           attn_sink, pool_w, pool_scale, w_out, w_ffn_in, w_ffn_out, final_g):
    b, n, d = x.shape
    n_ctx = ctx.shape[1]
    depth = w_mod.shape[0]
    heads = attn_sink.shape[1]
    attn_w = heads * HEAD_DIM
    conv_w = conv_dw.shape[2]
    pool_wd = pool_scale.shape[1]
    kv_w = w_in.shape[2] - attn_w - 2 * conv_w - pool_wd
    kv_w //= 2
    dims = (attn_w, 2 * kv_w, conv_w, pool_wd)

    rows = -(-(b + 1) // SUBLANES) * SUBLANES
    cc = jnp.zeros((rows, d), F32).at[:b].set(c).at[b].set(c_ctx)
    mod_all = _modulation(cc, w_mod, b_mod)

    tables = _rope_tables(n)
    tm_in = min(512, n)
    tm_mix = min(256, n)
    cx = ctx
    for l in range(depth):
        last = l == depth - 1
        mod = mod_all[l, :b].reshape(b, 6, d)
        mod_c = jnp.broadcast_to(mod_all[l, b].reshape(1, 6, d), (b, 6, d))
        wi = w_in[l]
        wq, wk, wv = wi[:, :attn_w], wi[:, attn_w:attn_w + kv_w], wi[:, attn_w + kv_w:attn_w + 2 * kv_w]
        w_kv = jnp.concatenate([_dup_heads(wk), _dup_heads(wv)], axis=1)
        w_all = jnp.concatenate([wq, w_kv, wi[:, attn_w + 2 * kv_w:]], axis=1).astype(BF16)
        g1 = norm1_g[l].reshape(1, d)
        g2 = norm2_g[l].reshape(1, d)
        cvec = jnp.zeros((SUBLANES, conv_w), F32).at[0].set(conv_dw_b[l]).at[1].set(conv_ln_g[l]) \
            .at[2].set(conv_ln_b[l]).at[3].set(pool_scale[l])
        poolw = jax.scipy.linalg.block_diag(*[pool_w[l, gi] for gi in range(pool_w.shape[1])]).astype(BF16)
        shared = (attn_sink[l], conv_dw[l], cvec, poolw, w_out[l].astype(BF16), g2,
                  w_ffn_in[l].astype(BF16), w_ffn_out[l].astype(BF16), final_g.reshape(1, d))

        q, kk, vv, h, p = _inproj(x, mod, g1, w_all, tables, tm=tm_in, kv_only=False, dims=dims)
        if last:
            kc, vc = _inproj(cx, mod_c, g1, w_kv.astype(BF16), None, tm=n_ctx, kv_only=True, dims=dims)
        else:
            qc, kc, vc, hc, pc = _inproj(cx, mod_c, g1, w_all, None, tm=n_ctx, kv_only=False, dims=dims)
        x = _mixer(x, q, kk, vv, kc, vc, h, p, mod, *shared, tm=tm_mix, final_norm=last)
        if not last:
            cx = _mixer(cx, qc, None, None, kc, vc, hc, pc, mod_c, *shared, tm=n_ctx, final_norm=False)
    return x
```

```python
import functools

import jax
import jax.numpy as jnp
from jax import lax
from jax.experimental import pallas as pl
from jax.experimental.pallas import tpu as pltpu

F32 = jnp.float32
BF16 = jnp.bfloat16

LANES = 128
SUBLANES = 8
MXU_COLS = 256
FFN_CHUNK_TILES = 3
VMEM_LIMIT_BYTES = 56 * 1024 * 1024

GRID_W = 64
HEAD_DIM = 64
ROPE_BASE = 10000.0
WINDOW = 128
Q_BLOCK = 128
SPAN = Q_BLOCK + 2 * WINDOW
CONV_KERNEL = 31
CONV_PAD = CONV_KERNEL // 2
CONV_HALO = 16
POOL_WINDOWS = (2, 4, 8, 16)
POOL_HALO = 8
ROW_CHUNK = 64
EPS = 1e-6
NEG = -1e30


def _rms_norm(x, g):
    return x * lax.rsqrt(jnp.mean(x * x, axis=-1, keepdims=True) + EPS) * g


def _dot(a, b):
    return jnp.dot(a, b, preferred_element_type=F32)


def _dot_nt(a, b):
    return lax.dot_general(a, b, (((1,), (1,)), ((), ())), preferred_element_type=F32)


def _mod_kernel(c_ref, w_ref, b_ref, o_ref):
    c = c_ref[...]
    a = c * jax.nn.sigmoid(c)
    o_ref[0] = jnp.dot(a, w_ref[0], preferred_element_type=F32,
                       precision=lax.Precision.HIGHEST) + b_ref[0]


def _modulation(cc, w_mod, b_mod):
    depth, d, d6 = w_mod.shape
    rows = cc.shape[0]
    tn = d6 // 4
    return pl.pallas_call(
        _mod_kernel,
        grid=(depth, d6 // tn),
        in_specs=[
            pl.BlockSpec((rows, d), lambda l, j: (0, 0)),
            pl.BlockSpec((1, d, tn), lambda l, j: (l, 0, j)),
            pl.BlockSpec((1, 1, tn), lambda l, j: (l, 0, j)),
        ],
        out_specs=pl.BlockSpec((1, rows, tn), lambda l, j: (l, 0, j)),
        out_shape=jax.ShapeDtypeStruct((depth, rows, d6), F32),
        compiler_params=pltpu.CompilerParams(
            dimension_semantics=("arbitrary", "arbitrary"), vmem_limit_bytes=VMEM_LIMIT_BYTES),
        name="modulation",
    )(cc, w_mod, b_mod.reshape(depth, 1, d6))


def _rope(t, cos, sin_a, sin_b):
    quarter = HEAD_DIM // 4
    return (t * cos + pltpu.roll(t, LANES - quarter, 1) * sin_a + pltpu.roll(t, quarter, 1) * sin_b)


def _inproj_kernel(*refs, rope, kv_only, attn_w, kv_w, conv_w):
    if rope:
        x_ref, mod_ref, g_ref, w_ref, cos_ref, sa_ref, sb_ref = refs[:7]
        outs = refs[7:]
    else:
        x_ref, mod_ref, g_ref, w_ref = refs[:4]
        outs = refs[4:]
    x = x_ref[0]
    hl = _rms_norm(x, g_ref[...]) * (1.0 + mod_ref[0, 1:2, :]) + mod_ref[0, 0:1, :]
    u = _dot(hl.astype(BF16), w_ref[...])

    def maybe_rope(t):
        if rope:
            return _rope(t, cos_ref[...], sa_ref[...], sb_ref[...])
        return t

    if kv_only:
        kk_ref, vv_ref = outs
        off = 0
    else:
        q_ref, kk_ref, vv_ref, h_ref, p_ref = outs
        for c in range(attn_w // LANES):
            t = maybe_rope(u[:, c * LANES:(c + 1) * LANES])
            q_ref[0, :, c * LANES:(c + 1) * LANES] = (t * (HEAD_DIM ** -0.5)).astype(BF16)
        off = attn_w
    for c in range(kv_w // LANES):
        t = maybe_rope(u[:, off + c * LANES: off + (c + 1) * LANES])
        kk_ref[0, :, c * LANES:(c + 1) * LANES] = t.astype(BF16)
    off += kv_w
    vv_ref[0] = u[:, off:off + kv_w].astype(BF16)
    off += kv_w
    if not kv_only:
        a = u[:, off:off + conv_w]
        g = u[:, off + conv_w:off + 2 * conv_w]
        h_ref[0] = a * jax.nn.sigmoid(g)
        off += 2 * conv_w
        p_ref[0] = u[:, off:]


def _inproj(x, mod, g, w, tables, *, tm, kv_only, dims):
    b, n, d = x.shape
    attn_w, kv_w, conv_w, pool_w = dims
    rope = tables is not None
    grid = (n // tm, b)
    in_specs = [
        pl.BlockSpec((1, tm, d), lambda i, bb: (bb, i, 0)),
        pl.BlockSpec((1, 6, d), lambda i, bb: (bb, 0, 0)),
        pl.BlockSpec((1, d), lambda i, bb: (0, 0)),
        pl.BlockSpec(w.shape, lambda i, bb: (0, 0)),
    ]
    args = [x, mod, g, w]
    if rope:
        in_specs += [pl.BlockSpec((tm, LANES), lambda i, bb: (i, 0))] * 3
        args += list(tables)
    tok = lambda width: pl.BlockSpec((1, tm, width), lambda i, bb: (bb, i, 0))
    out_specs = [tok(kv_w), tok(kv_w)]
    out_shape = [jax.ShapeDtypeStruct((b, n, kv_w), BF16)] * 2
    if not kv_only:
        out_specs = [tok(attn_w)] + out_specs + [tok(conv_w), tok(pool_w)]
        out_shape = ([jax.ShapeDtypeStruct((b, n, attn_w), BF16)] + out_shape
                     + [jax.ShapeDtypeStruct((b, n, conv_w), F32), jax.ShapeDtypeStruct((b, n, pool_w), F32)])
    kern = functools.partial(_inproj_kernel, rope=rope, kv_only=kv_only,
                             attn_w=attn_w, kv_w=kv_w, conv_w=conv_w)
    return pl.pallas_call(
        kern, grid=grid, in_specs=in_specs, out_specs=out_specs, out_shape=out_shape,
        compiler_params=pltpu.CompilerParams(
            dimension_semantics=("arbitrary", "arbitrary"), vmem_limit_bytes=VMEM_LIMIT_BYTES),
        name="inproj_kv" if kv_only else "inproj",
    )(*args)


def _ffn_chunks(d_ff):
    unit = MXU_COLS if d_ff % MXU_COLS == 0 else LANES
    units = d_ff // unit
    per = FFN_CHUNK_TILES
    return [(u * unit, min(per, units - u) * unit) for u in range(0, units, per)]


def _attention_scores(q2, halves, lmask):
    scores = []
    for kcm, _, klm, _ in halves:
        s_c = _dot_nt(q2, kcm)
        s_l = None if klm is None else jnp.where(lmask, _dot_nt(q2, klm), NEG)
        scores.append((s_c, s_l))
    return scores


def _attention_output(scores, halves, sinks):
    out = None
    for (s_c, s_l), (_, vcm, _, vlm), sink in zip(scores, halves, sinks):
        mx = jnp.maximum(jnp.max(s_c, axis=-1, keepdims=True), sink)
        if s_l is not None:
            mx = jnp.maximum(mx, jnp.max(s_l, axis=-1, keepdims=True))
        e_c = jnp.exp(s_c - mx)
        den = jnp.exp(sink - mx) + jnp.sum(e_c, axis=-1, keepdims=True)
        o = _dot(e_c.astype(BF16), vcm)
        if s_l is not None:
            e_l = jnp.exp(s_l - mx)
            den = den + jnp.sum(e_l, axis=-1, keepdims=True)
            o = o + _dot(e_l.astype(BF16), vlm)
        o = o * (1.0 / den)
        out = o if out is None else out + o
    return out


def _mixer_kernel(*refs, n, tm, has_local, final_norm, attn_w, conv_w, pool_w, d_ff):
    it = iter(refs)
    x_ref, q_ref = next(it), next(it)
    if has_local:
        kk_ref, vv_ref = next(it), next(it)
    kc_ref, vc_ref = next(it), next(it)
    h_ref, p_ref, mod_ref, sink_ref = next(it), next(it), next(it), next(it)
    dw_ref, cvec_ref, poolw_ref, wout_ref, g2_ref, wfi_ref, wfo_ref, gf_ref = (next(it) for _ in range(8))
    out_ref = next(it)
    mix_ref, hp_ref, hs_ref, pp_ref, y_ref, y2_ref = (next(it) for _ in range(6))

    s = pl.program_id(0)
    tiles = pl.num_programs(0) - 1
    tiles_per_seq = n // tm
    i = lax.rem(jnp.minimum(s, tiles - 1), tiles_per_seq)
    last_i = tiles_per_seq - 1
    t0 = pl.multiple_of(i * tm, tm)

    @pl.when(s == 0)
    def _():
        mix_ref[...] = jnp.zeros_like(mix_ref)

    ffn_state = {}

    def out_project():
        ffn_state["o"] = _dot(mix_ref[...], wout_ref[...])

    def ffn_norm():
        x1 = x_ref[0] + mod_ref[0, 2:3, :] * ffn_state["o"]
        y2 = _rms_norm(x1, g2_ref[...]) * (1.0 + mod_ref[0, 4:5, :]) + mod_ref[0, 3:4, :]
        y2_ref[...] = y2.astype(BF16)
        ffn_state["x1"] = x1

    ffn_parts = []
    deferred = []

    def ffn_stages(a, w):
        state = {}

        def hidden():
            lhs = y2_ref[...]
            state["gate"] = _dot(lhs, wfi_ref[:, a:a + w])
            state["up"] = _dot(lhs, wfi_ref[:, d_ff + a:d_ff + a + w])

        def project():
            gate = state["gate"]
            act = (gate * jax.nn.sigmoid(gate) * state["up"]).astype(BF16)
            part = _dot(act, wfo_ref[a:a + w, :])
            ffn_parts[:] = [part if not ffn_parts else ffn_parts[0] + part]
        return hidden, project

    ffn_pieces = [ffn_stages(a, w) for a, w in _ffn_chunks(d_ff)]


    lane1 = lax.broadcasted_iota(jnp.int32, (1, LANES), 1)
    low = lane1 < HEAD_DIM
    zero = jnp.zeros((), BF16)
    n_groups = kc_ref.shape[2] // LANES
    heads_per_group = attn_w // HEAD_DIM // n_groups

    def attention_stages(blk, g, pr):
        state = {}
        rows = slice(blk * Q_BLOCK, (blk + 1) * Q_BLOCK)
        col = (g * heads_per_group // 2 + pr) * LANES
        head = g * heads_per_group + 2 * pr

        def scores():
            gs = slice(g * LANES, (g + 1) * LANES)
            kc, vc = kc_ref[0, :, gs], vc_ref[0, :, gs]
            if has_local:
                start = t0 + blk * Q_BLOCK
                s0 = pl.multiple_of(jnp.clip(start - WINDOW, 0, n - SPAN), Q_BLOCK)
                r = lax.broadcasted_iota(jnp.int32, (Q_BLOCK, SPAN), 0)
                c = lax.broadcasted_iota(jnp.int32, (Q_BLOCK, SPAN), 1)
                dist = (start - s0) + r - c
                lmask = (dist >= -WINDOW) & (dist <= WINDOW)
                kl = kk_ref[0, pl.ds(s0, SPAN), gs]
                vl = vv_ref[0, pl.ds(s0, SPAN), gs]
            else:
                kl = vl = lmask = None
            halves = []
            for m in (low, jnp.logical_not(low)):
                sel = lambda t: None if t is None else jnp.where(m, t, zero)
                halves.append((sel(kc), sel(vc), sel(kl), sel(vl)))
            state["halves"] = halves
            state["scores"] = _attention_scores(q_ref[0, rows, col:col + LANES], halves, lmask)

        def output():
            out = _attention_output(state["scores"], state["halves"], (sink_ref[head], sink_ref[head + 1]))
            deferred.append(((rows, slice(col, col + LANES)), out.astype(BF16)))
        return scores, output

    attention_pieces = [attention_stages(blk, g, pr) for blk in range(tm // Q_BLOCK)
                        for g in range(n_groups) for pr in range(heads_per_group // 2)]

    dw_b, ln_g, ln_b, pool_scale = (cvec_ref[k:k + 1, :] for k in range(4))

    def conv_prepare():
        hp_ref[CONV_HALO:CONV_HALO + tm, :] = h_ref[0, pl.ds(t0, tm), :]
        prev = h_ref[0, pl.ds(pl.multiple_of(jnp.maximum(t0 - CONV_HALO, 0), CONV_HALO), CONV_HALO), :]
        hp_ref[0:CONV_HALO, :] = jnp.where(i > 0, prev, 0.0)
        nxt = h_ref[0, pl.ds(pl.multiple_of(jnp.minimum(t0 + tm, n - CONV_HALO), CONV_HALO), CONV_HALO), :]
        hp_ref[CONV_HALO + tm:, :] = jnp.where(i < last_i, nxt, 0.0)
        shifted_rows = tm + 2 * CONV_HALO - SUBLANES
        for sh in range(1, SUBLANES):
            hs_ref[sh - 1] = hp_ref[sh:sh + shifted_rows, :]

    def conv_piece(r0):
        def run():
            acc = jnp.zeros((ROW_CHUNK, conv_w), F32)
            for k in range(CONV_KERNEL):
                base = r0 + CONV_HALO - CONV_PAD + k
                sh = base % SUBLANES
                if sh == 0:
                    tap = hp_ref[base:base + ROW_CHUNK, :]
                else:
                    tap = hs_ref[sh - 1, base - sh:base - sh + ROW_CHUNK, :]
                acc = acc + tap * dw_ref[k:k + 1, :]
            hc = acc + dw_b
            mu = jnp.mean(hc, axis=-1, keepdims=True)
            cen = hc - mu
            var = jnp.mean(cen * cen, axis=-1, keepdims=True)
            hn = cen * lax.rsqrt(var + EPS) * ln_g + ln_b
            res = hn * jax.nn.sigmoid(hn)
            deferred.append(((slice(r0, r0 + ROW_CHUNK), slice(attn_w, attn_w + conv_w)), res.astype(BF16)))
        return run

    def pool_prepare():
        pp_ref[POOL_HALO:POOL_HALO + tm, :] = p_ref[0, pl.ds(t0, tm), :]
        prev = p_ref[0, pl.ds(pl.multiple_of(jnp.maximum(t0 - POOL_HALO, 0), POOL_HALO), POOL_HALO), :]
        pp_ref[0:POOL_HALO, :] = jnp.where(i > 0, prev, 0.0)
        nxt = p_ref[0, pl.ds(pl.multiple_of(jnp.minimum(t0 + tm, n - POOL_HALO), POOL_HALO), POOL_HALO), :]
        pp_ref[POOL_HALO + tm:, :] = jnp.where(i < last_i, nxt, 0.0)

    group_w = pool_w // len(POOL_WINDOWS)
    groups_per_col = LANES // group_w

    def pool_piece(r0):
        def run():
            lane = lax.broadcasted_iota(jnp.int32, (ROW_CHUNK, LANES), 1)
            row = lax.broadcasted_iota(jnp.int32, (ROW_CHUNK, LANES), 0)
            t = t0 + r0 + row
            for col in range(pool_w // LANES):
                cs = slice(col * LANES, (col + 1) * LANES)
                wins = POOL_WINDOWS[col * groups_per_col:(col + 1) * groups_per_col]
                total = None
                cnt = None
                for gi, win in enumerate(wins):
                    in_group = (lane >= gi * group_w) & (lane < (gi + 1) * group_w)
                    back, fwd = win // 2, win - 1 - win // 2
                    lo = jnp.maximum(t - back, 0)
                    hi = jnp.minimum(t + fwd, n - 1)
                    c_g = (hi - lo + 1).astype(F32)
                    cnt = c_g if cnt is None else jnp.where(in_group, c_g, cnt)
                for off in range(-POOL_HALO, POOL_HALO):
                    member = [(-(w // 2) <= off <= w - 1 - w // 2) for w in wins]
                    if not any(member):
                        continue
                    base = r0 + POOL_HALO + off
                    val = pp_ref[base:base + ROW_CHUNK, cs]
                    if not all(member):
                        keep = None
                        for gi, mem in enumerate(member):
                            if mem:
                                in_group = (lane >= gi * group_w) & (lane < (gi + 1) * group_w)
                                keep = in_group if keep is None else keep | in_group
                        val = jnp.where(keep, val, 0.0)
                    total = val if total is None else total + val
                centre = pp_ref[r0 + POOL_HALO:r0 + POOL_HALO + ROW_CHUNK, cs]
                y_ref[r0:r0 + ROW_CHUNK, cs] = (total / cnt - centre).astype(BF16)
        return run

    def pool_project():
        pooled = _dot(y_ref[...], poolw_ref[...]) * pool_scale
        deferred.append(((slice(None), slice(attn_w + conv_w, None)), pooled.astype(BF16)))

    n_chunks = len(ffn_pieces)
    per_group = -(-len(attention_pieces) // n_chunks)
    groups = [attention_pieces[j * per_group:(j + 1) * per_group] for j in range(n_chunks)]
    score_stage = lambda j: [scores for scores, _ in groups[j]]
    output_stage = lambda j: [output for _, output in groups[j]]
    hidden_stage = lambda j: [ffn_pieces[j][0]]
    project_stage = lambda j: [ffn_pieces[j][1]]
    elementwise = ([conv_prepare] + [conv_piece(r0) for r0 in range(0, tm, ROW_CHUNK)]
                   + [pool_prepare] + [pool_piece(r0) for r0 in range(0, tm, ROW_CHUNK)])
    per_elem = -(-len(elementwise) // n_chunks)
    order = [out_project, ffn_norm]
    for j in range(n_chunks):
        order += score_stage(j) + hidden_stage(j) + elementwise[j * per_elem:(j + 1) * per_elem]
        order += output_stage(j) + project_stage(j)
    order += [pool_project]
    for stage in order:
        stage()

    x2 = ffn_state["x1"] + mod_ref[0, 5:6, :] * ffn_parts[0]
    if final_norm:
        x2 = _rms_norm(x2, gf_ref[...])
    out_ref[0] = x2
    for index, value in deferred:
        mix_ref[index] = value


def _mixer(x, q, kk, vv, kc, vc, h, p, mod, sink, dw, cvec, poolw, wout, g2, wfi, wfo, gf,
           *, tm, final_norm):
    b, n, d = x.shape
    has_local = kk is not None
    attn_w, kv_w, conv_w, pool_w = q.shape[2], kc.shape[2], h.shape[2], p.shape[2]
    n_ctx = kc.shape[1]
    d_ff = wfo.shape[0]
    tiles_per_seq = n // tm
    tiles = b * tiles_per_seq
    mix_tile = lambda s: jnp.minimum(s, tiles - 1)
    ffn_tile = lambda s: jnp.maximum(s - 1, 0)
    const = lambda shape: pl.BlockSpec(shape, lambda s: (0,) * len(shape), pipeline_mode=pl.Buffered(1))
    tok = lambda width, tile: pl.BlockSpec(
        (1, tm, width), lambda s: (tile(s) // tiles_per_seq, tile(s) % tiles_per_seq, 0))
    seq = lambda length, width, tile: pl.BlockSpec((1, length, width), lambda s: (tile(s) // tiles_per_seq, 0, 0))
    in_specs = [tok(d, ffn_tile), tok(attn_w, mix_tile)]
    args = [x, q]
    if has_local:
        in_specs += [seq(n, kv_w, mix_tile), seq(n, kv_w, mix_tile)]
        args += [kk, vv]
    in_specs += [seq(n_ctx, kv_w, mix_tile), seq(n_ctx, kv_w, mix_tile), seq(n, conv_w, mix_tile),
                 seq(n, pool_w, mix_tile), seq(6, d, ffn_tile),
                 pl.BlockSpec(memory_space=pltpu.SMEM),
                 const(dw.shape), const(cvec.shape), const(poolw.shape), const(wout.shape),
                 const(g2.shape), const(wfi.shape), const(wfo.shape), const(gf.shape)]
    args += [kc, vc, h, p, mod, sink, dw, cvec, poolw, wout, g2, wfi, wfo, gf]
    kern = functools.partial(_mixer_kernel, n=n, tm=tm, has_local=has_local, final_norm=final_norm,
                             attn_w=attn_w, conv_w=conv_w, pool_w=pool_w, d_ff=d_ff)
    return pl.pallas_call(
        kern, grid=(tiles + 1,), in_specs=in_specs, out_specs=tok(d, ffn_tile),
        out_shape=jax.ShapeDtypeStruct((b, n, d), F32),
        scratch_shapes=[
            pltpu.VMEM((tm, attn_w + conv_w + pool_w), BF16),
            pltpu.VMEM((tm + 2 * CONV_HALO, conv_w), F32),
            pltpu.VMEM((SUBLANES - 1, tm + 2 * CONV_HALO - SUBLANES, conv_w), F32),
            pltpu.VMEM((tm + 2 * POOL_HALO, pool_w), F32),
            pltpu.VMEM((tm, pool_w), BF16),
            pltpu.VMEM((tm, d), BF16),
        ],
        compiler_params=pltpu.CompilerParams(
            dimension_semantics=("arbitrary",), vmem_limit_bytes=VMEM_LIMIT_BYTES),
        name="mixer_local" if has_local else "mixer_ctx",
    )(*args)


def _rope_tables(n):
    rows = n // GRID_W
    row = jnp.repeat(jnp.arange(rows), GRID_W).astype(F32)
    col = jnp.tile(jnp.arange(GRID_W), rows).astype(F32)
    half = HEAD_DIM // 2
    inv = ROPE_BASE ** (-jnp.arange(0, half, 2, dtype=F32) / half)
    ar = row[:, None] * inv
    ac = col[:, None] * inv
    ang = jnp.concatenate([ar, ar, ac, ac], axis=-1)
    cos, sin = jnp.cos(ang), jnp.sin(ang)
    first = (jnp.arange(HEAD_DIM) // (HEAD_DIM // 4)) % 2 == 0
    sin_a = jnp.where(first, -sin, 0.0)
    sin_b = jnp.where(first, 0.0, sin)
    rep = LANES // HEAD_DIM
    return tuple(jnp.tile(t, (1, rep)) for t in (cos, sin_a, sin_b))


def _dup_heads(w):
    d, width = w.shape
    w = w.reshape(d, width // HEAD_DIM, 1, HEAD_DIM)
    return jnp.broadcast_to(w, (d, width // HEAD_DIM, 2, HEAD_DIM)).reshape(d, 2 * width)


def kernel(x, c, ctx, c_ctx, w_mod, b_mod, norm1_g, norm2_g, w_in, conv_dw, conv_dw_b, conv_ln_g, conv_ln_b,
           attn_sink, pool_w, pool_scale, w_out, w_ffn_in, w_ffn_out, final_g):
    b, n, d = x.shape
    n_ctx = ctx.shape[1]
    depth = w_mod.shape[0]
    heads = attn_sink.shape[1]
    attn_w = heads * HEAD_DIM
    conv_w = conv_dw.shape[2]
    pool_wd = pool_scale.shape[1]
    kv_w = w_in.shape[2] - attn_w - 2 * conv_w - pool_wd
    kv_w //= 2
    dims = (attn_w, 2 * kv_w, conv_w, pool_wd)

    rows = -(-(b + 1) // SUBLANES) * SUBLANES
    cc = jnp.zeros((rows, d), F32).at[:b].set(c).at[b].set(c_ctx)
    mod_all = _modulation(cc, w_mod, b_mod)

    tables = _rope_tables(n)
    tm_in = min(512, n)
    tm_mix = min(256, n)
    cx = ctx
    for l in range(depth):
        last = l == depth - 1
        mod = mod_all[l, :b].reshape(b, 6, d)
        mod_c = jnp.broadcast_to(mod_all[l, b].reshape(1, 6, d), (b, 6, d))
        wi = w_in[l]
        wq, wk, wv = wi[:, :attn_w], wi[:, attn_w:attn_w + kv_w], wi[:, attn_w + kv_w:attn_w + 2 * kv_w]
        w_kv = jnp.concatenate([_dup_heads(wk), _dup_heads(wv)], axis=1)
        w_all = jnp.concatenate([wq, w_kv, wi[:, attn_w + 2 * kv_w:]], axis=1).astype(BF16)
        g1 = norm1_g[l].reshape(1, d)
        g2 = norm2_g[l].reshape(1, d)
        cvec = jnp.zeros((SUBLANES, conv_w), F32).at[0].set(conv_dw_b[l]).at[1].set(conv_ln_g[l]) \
            .at[2].set(conv_ln_b[l]).at[3].set(pool_scale[l])
        poolw = jax.scipy.linalg.block_diag(*[pool_w[l, gi] for gi in range(pool_w.shape[1])]).astype(BF16)
        shared = (attn_sink[l], conv_dw[l], cvec, poolw, w_out[l].astype(BF16), g2,
                  w_ffn_in[l].astype(BF16), w_ffn_out[l].astype(BF16), final_g.reshape(1, d))

        q, kk, vv, h, p = _inproj(x, mod, g1, w_all, tables, tm=tm_in, kv_only=False, dims=dims)
        if last:
            kc, vc = _inproj(cx, mod_c, g1, w_kv.astype(BF16), None, tm=n_ctx, kv_only=True, dims=dims)
        else:
            qc, kc, vc, hc, pc = _inproj(cx, mod_c, g1, w_all, None, tm=n_ctx, kv_only=False, dims=dims)
        x = _mixer(x, q, kk, vv, kc, vc, h, p, mod, *shared, tm=tm_mix, final_norm=last)
        if not last:
            cx = _mixer(cx, qc, None, None, kc, vc, hc, pc, mod_c, *shared, tm=n_ctx, final_norm=False)
    return x
```

```python
import functools

import jax
import jax.numpy as jnp
from jax import lax
from jax.experimental import pallas as pl
from jax.experimental.pallas import tpu as pltpu

F32 = jnp.float32
BF16 = jnp.bfloat16

LANES = 128
SUBLANES = 8
MXU_COLS = 256
FFN_CHUNK_TILES = 3
VMEM_LIMIT_BYTES = 56 * 1024 * 1024

GRID_W = 64
HEAD_DIM = 64
ROPE_BASE = 10000.0
WINDOW = 128
Q_BLOCK = 128
SPAN = Q_BLOCK + 2 * WINDOW
CONV_KERNEL = 31
CONV_PAD = CONV_KERNEL // 2
CONV_HALO = 16
POOL_WINDOWS = (2, 4, 8, 16)
POOL_HALO = 8
ROW_CHUNK = 64
EPS = 1e-6
NEG = -1e30
LOG2E = 1.4426950408889634


def _rms_norm(x, g):
    return x * lax.rsqrt(jnp.mean(x * x, axis=-1, keepdims=True) + EPS) * g


def _dot(a, b):
    return jnp.dot(a, b, preferred_element_type=F32)


def _dot_nt(a, b):
    return lax.dot_general(a, b, (((1,), (1,)), ((), ())), preferred_element_type=F32)


def _mod_kernel(c_ref, w_ref, b_ref, o_ref):
    c = c_ref[...]
    a = c * jax.nn.sigmoid(c)
    o_ref[0] = jnp.dot(a, w_ref[0], preferred_element_type=F32,
                       precision=lax.Precision.HIGHEST) + b_ref[0]


def _modulation(cc, w_mod, b_mod):
    depth, d, d6 = w_mod.shape
    rows = cc.shape[0]
    tn = d6 // 4
    return pl.pallas_call(
        _mod_kernel,
        grid=(depth, d6 // tn),
        in_specs=[
            pl.BlockSpec((rows, d), lambda l, j: (0, 0)),
            pl.BlockSpec((1, d, tn), lambda l, j: (l, 0, j)),
            pl.BlockSpec((1, 1, tn), lambda l, j: (l, 0, j)),
        ],
        out_specs=pl.BlockSpec((1, rows, tn), lambda l, j: (l, 0, j)),
        out_shape=jax.ShapeDtypeStruct((depth, rows, d6), F32),
        compiler_params=pltpu.CompilerParams(
            dimension_semantics=("arbitrary", "arbitrary"), vmem_limit_bytes=VMEM_LIMIT_BYTES),
        name="modulation",
    )(cc, w_mod, b_mod.reshape(depth, 1, d6))


def _rope(t, cos, sin_a, sin_b):
    quarter = HEAD_DIM // 4
    return (t * cos + pltpu.roll(t, LANES - quarter, 1) * sin_a + pltpu.roll(t, quarter, 1) * sin_b)


def _inproj_kernel(*refs, rope, kv_only, attn_w, kv_w, conv_w):
    if rope:
        x_ref, mod_ref, g_ref, w_ref, cos_ref, sa_ref, sb_ref = refs[:7]
        outs = refs[7:]
    else:
        x_ref, mod_ref, g_ref, w_ref = refs[:4]
        outs = refs[4:]
    x = x_ref[0]
    hl = _rms_norm(x, g_ref[...] * (1.0 + mod_ref[0, 1:2, :])) + mod_ref[0, 0:1, :]
    u = _dot(hl.astype(BF16), w_ref[...])

    def maybe_rope(t):
        if rope:
            return _rope(t, cos_ref[...], sa_ref[...], sb_ref[...])
        return t

    if kv_only:
        kk_ref, vv_ref = outs
        off = 0
    else:
        q_ref, kk_ref, vv_ref, h_ref, p_ref = outs
        for c in range(attn_w // LANES):
            t = maybe_rope(u[:, c * LANES:(c + 1) * LANES])
            q_ref[0, :, c * LANES:(c + 1) * LANES] = (t * (HEAD_DIM ** -0.5 * LOG2E)).astype(BF16)
        off = attn_w
    low = lax.broadcasted_iota(jnp.int32, (1, LANES), 1) < HEAD_DIM
    for c in range(kv_w // LANES):
        k = maybe_rope(u[:, off + c * LANES: off + (c + 1) * LANES])
        v = u[:, off + kv_w + c * LANES: off + kv_w + (c + 1) * LANES]
        for half, keep in enumerate((low, jnp.logical_not(low))):
            cols = slice((2 * c + half) * LANES, (2 * c + half + 1) * LANES)
            kk_ref[0, :, cols] = jnp.where(keep, k, 0.0).astype(BF16)
            vv_ref[0, :, cols] = jnp.where(keep, v, 0.0).astype(BF16)
    off += 2 * kv_w
    if not kv_only:
        a = u[:, off:off + conv_w]
        g = u[:, off + conv_w:off + 2 * conv_w]
        h_ref[0] = a * jax.nn.sigmoid(g)
        off += 2 * conv_w
        p_ref[0] = u[:, off:]


def _inproj(x, mod, g, w, tables, *, tm, kv_only, dims):
    b, n, d = x.shape
    attn_w, kv_w, conv_w, pool_w = dims
    rope = tables is not None
    grid = (n // tm, b)
    in_specs = [
        pl.BlockSpec((1, tm, d), lambda i, bb: (bb, i, 0)),
        pl.BlockSpec((1, 6, d), lambda i, bb: (bb, 0, 0)),
        pl.BlockSpec((1, d), lambda i, bb: (0, 0)),
        pl.BlockSpec(w.shape, lambda i, bb: (0, 0)),
    ]
    args = [x, mod, g, w]
    if rope:
        in_specs += [pl.BlockSpec((tm, LANES), lambda i, bb: (i, 0))] * 3
        args += list(tables)
    tok = lambda width: pl.BlockSpec((1, tm, width), lambda i, bb: (bb, i, 0))
    out_specs = [tok(2 * kv_w), tok(2 * kv_w)]
    out_shape = [jax.ShapeDtypeStruct((b, n, 2 * kv_w), BF16)] * 2
    if not kv_only:
        out_specs = [tok(attn_w)] + out_specs + [tok(conv_w), tok(pool_w)]
        out_shape = ([jax.ShapeDtypeStruct((b, n, attn_w), BF16)] + out_shape
                     + [jax.ShapeDtypeStruct((b, n, conv_w), F32), jax.ShapeDtypeStruct((b, n, pool_w), F32)])
    kern = functools.partial(_inproj_kernel, rope=rope, kv_only=kv_only,
                             attn_w=attn_w, kv_w=kv_w, conv_w=conv_w)
    return pl.pallas_call(
        kern, grid=grid, in_specs=in_specs, out_specs=out_specs, out_shape=out_shape,
        compiler_params=pltpu.CompilerParams(
            dimension_semantics=("arbitrary", "arbitrary"), vmem_limit_bytes=VMEM_LIMIT_BYTES),
        name="inproj_kv" if kv_only else "inproj",
    )(*args)


def _ffn_tile(d_ff):
    return MXU_COLS if d_ff % MXU_COLS == 0 else LANES


def _ffn_chunks(d_ff):
    unit = _ffn_tile(d_ff)
    units = d_ff // unit
    per = FFN_CHUNK_TILES
    return [(u * unit, min(per, units - u) * unit) for u in range(0, units, per)]


def _order_token(v):
    return (v[0:2 * SUBLANES, 0:LANES] > 0).astype(F32) * 0.0


def _attention_scores(q2, keys, lmask):
    scores = []
    for kcm, klm in keys:
        s_c = _dot_nt(q2, kcm)
        s_l = None if klm is None else jnp.where(lmask, _dot_nt(q2, klm), NEG)
        scores.append((s_c, s_l))
    return scores


def _attention_output(scores, values, sinks):
    out = None
    for (s_c, s_l), (vcm, vlm), sink in zip(scores, values, sinks):
        mx = jnp.maximum(jnp.max(s_c, axis=-1, keepdims=True), sink)
        if s_l is not None:
            mx = jnp.maximum(mx, jnp.max(s_l, axis=-1, keepdims=True))
        e_c = jnp.exp2(s_c - mx)
        den = jnp.exp2(sink - mx) + jnp.sum(e_c, axis=-1, keepdims=True)
        o = _dot(e_c.astype(BF16), vcm)
        if s_l is not None:
            e_l = jnp.exp2(s_l - mx)
            den = den + jnp.sum(e_l, axis=-1, keepdims=True)
            o = o + _dot(e_l.astype(BF16), vlm)
        o = o * (1.0 / den)
        out = o if out is None else out + o
    return out


def _mixer_kernel(*refs, n, tm, has_local, final_norm, attn_w, conv_w, pool_w, d_ff):
    it = iter(refs)
    x_ref, q_ref = next(it), next(it)
    if has_local:
        kk_ref, vv_ref = next(it), next(it)
    kc_ref, vc_ref = next(it), next(it)
    h_ref, p_ref, mod_ref, sink_ref = next(it), next(it), next(it), next(it)
    dw_ref, cvec_ref, band_ref, poolw_ref, wout_ref, g2_ref, wfi_ref, wfo_ref, gf_ref = (next(it) for _ in range(9))
    out_ref = next(it)
    mix_ref, hp_ref, hs_ref, pp_ref, y_ref, y2_ref = (next(it) for _ in range(6))

    s = pl.program_id(0)
    tiles = pl.num_programs(0) - 1
    tiles_per_seq = n // tm
    i = lax.rem(jnp.minimum(s, tiles - 1), tiles_per_seq)
    last_i = tiles_per_seq - 1
    t0 = pl.multiple_of(i * tm, tm)

    @pl.when(s == 0)
    def _():
        mix_ref[...] = jnp.zeros_like(mix_ref)

    ffn_state = {}

    def out_project():
        ffn_state["o"] = _dot(mix_ref[...], pltpu.bitcast(wout_ref[...], BF16))

    def ffn_norm():
        x1 = x_ref[0] + mod_ref[0, 2:3, :] * ffn_state["o"]
        y2 = _rms_norm(x1, g2_ref[...] * (1.0 + mod_ref[0, 4:5, :])) + mod_ref[0, 3:4, :]
        y2_ref[...] = y2.astype(BF16)
        ffn_state["x1"] = x1

    ffn_parts = []
    deferred = []
    tokens = []

    def ffn_stages(a, w):
        state = {}

        def hidden():
            if tokens:
                corner = (slice(0, 2 * SUBLANES), slice(0, LANES))
                y2_ref[corner] = y2_ref[corner] + sum(tokens).astype(BF16)
                tokens.clear()
            state["gu"] = _dot(y2_ref[...], wfi_ref[:, 2 * a:2 * (a + w)])

        def project():
            gu = state["gu"]
            acts = []
            tile = _ffn_tile(d_ff)
            for t in range(0, 2 * w, 2 * tile):
                gate, up = gu[:, t:t + tile], gu[:, t + tile:t + 2 * tile]
                acts.append((gate * jax.nn.sigmoid(gate) * up).astype(BF16))
            act = acts[0] if len(acts) == 1 else jnp.concatenate(acts, axis=1)
            part = _dot(act, pltpu.bitcast(wfo_ref[a // 2:(a + w) // 2, :], BF16))
            ffn_parts[:] = [part if not ffn_parts else ffn_parts[0] + part]
        return hidden, project

    ffn_pieces = [ffn_stages(a, w) for a, w in _ffn_chunks(d_ff)]


    zero = jnp.zeros((), BF16)
    n_groups = kc_ref.shape[2] // (2 * LANES)
    heads_per_group = attn_w // HEAD_DIM // n_groups

    def attention_stages(blk, g, pr):
        state = {}
        rows = slice(blk * Q_BLOCK, (blk + 1) * Q_BLOCK)
        col = (g * heads_per_group // 2 + pr) * LANES
        head = g * heads_per_group + 2 * pr

        half_cols = [slice((2 * g + hh) * LANES, (2 * g + hh + 1) * LANES) for hh in range(2)]
        start = t0 + blk * Q_BLOCK
        s0 = pl.multiple_of(jnp.clip(start - WINDOW, 0, n - SPAN), Q_BLOCK) if has_local else None

        def scores():
            if has_local:
                r = lax.broadcasted_iota(jnp.int32, (Q_BLOCK, SPAN), 0)
                c = lax.broadcasted_iota(jnp.int32, (Q_BLOCK, SPAN), 1)
                dist = (start - s0) + r - c
                lmask = (dist >= -WINDOW) & (dist <= WINDOW)
            else:
                lmask = None
            keys = [(kc_ref[0, :, cs], kk_ref[0, pl.ds(s0, SPAN), cs] if has_local else None) for cs in half_cols]
            state["scores"] = _attention_scores(q_ref[0, rows, col:col + LANES], keys, lmask)

        def output():
            values = [(vc_ref[0, :, cs], vv_ref[0, pl.ds(s0, SPAN), cs] if has_local else None) for cs in half_cols]
            out = _attention_output(state["scores"], values, (sink_ref[head] * LOG2E, sink_ref[head + 1] * LOG2E))
            deferred.append(((rows, slice(col, col + LANES)), out.astype(BF16)))
        return scores, output

    attention_pieces = [attention_stages(blk, g, pr) for blk in range(tm // Q_BLOCK)
                        for g in range(n_groups) for pr in range(heads_per_group // 2)]

    dw_b, ln_g, ln_b, pool_scale = (cvec_ref[k:k + 1, :] for k in range(4))

    def conv_prepare():
        hp_ref[CONV_HALO:CONV_HALO + tm, :] = h_ref[0, pl.ds(t0, tm), :]
        prev = h_ref[0, pl.ds(pl.multiple_of(jnp.maximum(t0 - CONV_HALO, 0), CONV_HALO), CONV_HALO), :]
        hp_ref[0:CONV_HALO, :] = jnp.where(i > 0, prev, 0.0)
        nxt = h_ref[0, pl.ds(pl.multiple_of(jnp.minimum(t0 + tm, n - CONV_HALO), CONV_HALO), CONV_HALO), :]
        hp_ref[CONV_HALO + tm:, :] = jnp.where(i < last_i, nxt, 0.0)
        shifted_rows = tm + 2 * CONV_HALO - SUBLANES
        for sh in range(1, SUBLANES):
            hs_ref[sh - 1] = hp_ref[sh:sh + shifted_rows, :]

    def conv_piece(r0):
        def run():
            acc = jnp.zeros((ROW_CHUNK, conv_w), F32)
            for k in range(CONV_KERNEL):
                base = r0 + CONV_HALO - CONV_PAD + k
                sh = base % SUBLANES
                if sh == 0:
                    tap = hp_ref[base:base + ROW_CHUNK, :]
                else:
                    tap = hs_ref[sh - 1, base - sh:base - sh + ROW_CHUNK, :]
                acc = acc + tap * dw_ref[k:k + 1, :]
            hc = acc + dw_b
            mu = jnp.mean(hc, axis=-1, keepdims=True)
            cen = hc - mu
            var = jnp.mean(cen * cen, axis=-1, keepdims=True)
            hn = cen * lax.rsqrt(var + EPS) * ln_g + ln_b
            res = hn * jax.nn.sigmoid(hn)
            deferred.append(((slice(r0, r0 + ROW_CHUNK), slice(attn_w, attn_w + conv_w)), res.astype(BF16)))
            tokens.append(_order_token(res))
        return run

    def pool_prepare():
        pp_ref[POOL_HALO:POOL_HALO + tm, :] = p_ref[0, pl.ds(t0, tm), :]
        prev = p_ref[0, pl.ds(pl.multiple_of(jnp.maximum(t0 - POOL_HALO, 0), POOL_HALO), POOL_HALO), :]
        pp_ref[0:POOL_HALO, :] = jnp.where(i > 0, prev, 0.0)
        nxt = p_ref[0, pl.ds(pl.multiple_of(jnp.minimum(t0 + tm, n - POOL_HALO), POOL_HALO), POOL_HALO), :]
        pp_ref[POOL_HALO + tm:, :] = jnp.where(i < last_i, nxt, 0.0)

    group_w = pool_w // len(POOL_WINDOWS)
    span_rows = Q_BLOCK + 2 * POOL_HALO

    def pool_block(blk):
        def run():
            start = t0 + blk * Q_BLOCK
            which = jnp.where(start == 0, 0, jnp.where(start == n - Q_BLOCK, 2, 1))
            span = pp_ref[blk * Q_BLOCK:blk * Q_BLOCK + span_rows, :].astype(BF16)
            lane = lax.broadcasted_iota(jnp.int32, (1, pool_w), 1)
            y = None
            for gi in range(len(POOL_WINDOWS)):
                in_group = (lane >= gi * group_w) & (lane < (gi + 1) * group_w)
                part = _dot(band_ref[which, gi], jnp.where(in_group, span, zero))
                y = part if y is None else y + part
            y_ref[blk * Q_BLOCK:(blk + 1) * Q_BLOCK, :] = y.astype(BF16)
        return run

    def pool_project():
        pooled = _dot(y_ref[...], poolw_ref[...]) * pool_scale
        deferred.append(((slice(None), slice(attn_w + conv_w, None)), pooled.astype(BF16)))

    n_chunks = len(ffn_pieces)
    per_group = -(-len(attention_pieces) // n_chunks)
    groups = [attention_pieces[j * per_group:(j + 1) * per_group] for j in range(n_chunks)]
    score_stage = lambda j: [scores for scores, _ in groups[j]]
    output_stage = lambda j: [output for _, output in groups[j]]
    hidden_stage = lambda j: [ffn_pieces[j][0]]
    project_stage = lambda j: [ffn_pieces[j][1]]
    elementwise = ([conv_prepare] + [conv_piece(r0) for r0 in range(0, tm, ROW_CHUNK)]
                   + [pool_prepare] + [pool_block(blk) for blk in range(tm // Q_BLOCK)])
    per_elem = -(-len(elementwise) // max(n_chunks - 1, 1))
    order = [out_project, ffn_norm] + score_stage(0)
    for j in range(n_chunks):
        order += hidden_stage(j) + elementwise[j * per_elem:(j + 1) * per_elem] + output_stage(j)
        if j + 1 < n_chunks:
            order += score_stage(j + 1)
        order += project_stage(j)
    order += [pool_project]
    for stage in order:
        stage()

    x2 = ffn_state["x1"] + mod_ref[0, 5:6, :] * ffn_parts[0]
    if final_norm:
        x2 = _rms_norm(x2, gf_ref[...])
    out_ref[0] = x2
    for index, value in deferred:
        mix_ref[index] = value


def _mixer(x, q, kk, vv, kc, vc, h, p, mod, sink, dw, cvec, band, poolw, wout, g2, wfi, wfo, gf,
           *, tm, final_norm):
    b, n, d = x.shape
    has_local = kk is not None
    attn_w, kv_w, conv_w, pool_w = q.shape[2], kc.shape[2], h.shape[2], p.shape[2]
    n_ctx = kc.shape[1]
    d_ff = 2 * wfo.shape[0]
    tiles_per_seq = n // tm
    tiles = b * tiles_per_seq
    mix_tile = lambda s: jnp.minimum(s, tiles - 1)
    ffn_tile = lambda s: jnp.maximum(s - 1, 0)
    const = lambda shape: pl.BlockSpec(shape, lambda s: (0,) * len(shape), pipeline_mode=pl.Buffered(1))
    tok = lambda width, tile: pl.BlockSpec(
        (1, tm, width), lambda s: (tile(s) // tiles_per_seq, tile(s) % tiles_per_seq, 0))
    seq = lambda length, width, tile: pl.BlockSpec((1, length, width), lambda s: (tile(s) // tiles_per_seq, 0, 0))
    in_specs = [tok(d, ffn_tile), tok(attn_w, mix_tile)]
    args = [x, q]
    if has_local:
        in_specs += [seq(n, kv_w, mix_tile), seq(n, kv_w, mix_tile)]
        args += [kk, vv]
    in_specs += [seq(n_ctx, kv_w, mix_tile), seq(n_ctx, kv_w, mix_tile), seq(n, conv_w, mix_tile),
                 seq(n, pool_w, mix_tile), seq(6, d, ffn_tile),
                 pl.BlockSpec(memory_space=pltpu.SMEM),
                 const(dw.shape), const(cvec.shape), const(band.shape), const(poolw.shape), const(wout.shape),
                 const(g2.shape), const(wfi.shape), const(wfo.shape), const(gf.shape)]
    args += [kc, vc, h, p, mod, sink, dw, cvec, band, poolw, wout, g2, wfi, wfo, gf]
    kern = functools.partial(_mixer_kernel, n=n, tm=tm, has_local=has_local, final_norm=final_norm,
                             attn_w=attn_w, conv_w=conv_w, pool_w=pool_w, d_ff=d_ff)
    return pl.pallas_call(
        kern, grid=(tiles + 1,), in_specs=in_specs, out_specs=tok(d, ffn_tile),
        out_shape=jax.ShapeDtypeStruct((b, n, d), F32),
        scratch_shapes=[
            pltpu.VMEM((tm, attn_w + conv_w + pool_w), BF16),
            pltpu.VMEM((tm + 2 * CONV_HALO, conv_w), F32),
            pltpu.VMEM((SUBLANES - 1, tm + 2 * CONV_HALO - SUBLANES, conv_w), F32),
            pltpu.VMEM((tm + 2 * POOL_HALO, pool_w), F32),
            pltpu.VMEM((tm, pool_w), BF16),
            pltpu.VMEM((tm, d), BF16),
        ],
        compiler_params=pltpu.CompilerParams(
            dimension_semantics=("arbitrary",), vmem_limit_bytes=VMEM_LIMIT_BYTES),
        name="mixer_local" if has_local else "mixer_ctx",
    )(*args)


def _rope_tables(n):
    rows = n // GRID_W
    row = jnp.repeat(jnp.arange(rows), GRID_W).astype(F32)
    col = jnp.tile(jnp.arange(GRID_W), rows).astype(F32)
    half = HEAD_DIM // 2
    inv = ROPE_BASE ** (-jnp.arange(0, half, 2, dtype=F32) / half)
    ar = row[:, None] * inv
    ac = col[:, None] * inv
    ang = jnp.concatenate([ar, ar, ac, ac], axis=-1)
    cos, sin = jnp.cos(ang), jnp.sin(ang)
    first = (jnp.arange(HEAD_DIM) // (HEAD_DIM // 4)) % 2 == 0
    sin_a = jnp.where(first, -sin, 0.0)
    sin_b = jnp.where(first, 0.0, sin)
    rep = LANES // HEAD_DIM
    return tuple(jnp.tile(t, (1, rep)) for t in (cos, sin_a, sin_b))


def _pool_bands(n):
    assert n >= 2 * Q_BLOCK and n % Q_BLOCK == 0
    r = jnp.arange(Q_BLOCK)[:, None]
    c = jnp.arange(Q_BLOCK + 2 * POOL_HALO)[None, :]
    variants = []
    for start in (0, Q_BLOCK if n > 2 * Q_BLOCK else 0, n - Q_BLOCK):
        t, pos = start + r, start - POOL_HALO + c
        per_window = []
        for win in POOL_WINDOWS:
            lo = jnp.maximum(t - win // 2, 0)
            hi = jnp.minimum(t + win - 1 - win // 2, n - 1)
            inside = (pos >= lo) & (pos <= hi)
            mean = jnp.where(inside, 1.0 / (hi - lo + 1).astype(F32), 0.0)
            per_window.append(mean - (pos == t).astype(F32))
        variants.append(jnp.stack(per_window))
    return jnp.stack(variants).astype(BF16)


def _interleave_gate_up(w):
    d, two_ff = w.shape
    d_ff = two_ff // 2
    tile = _ffn_tile(d_ff)
    return w.reshape(d, 2, d_ff // tile, tile).transpose(0, 2, 1, 3).reshape(d, two_ff)


def _pack_row_pairs(w):
    k, n = w.shape
    return lax.bitcast_convert_type(w.reshape(k // 2, 2, n).transpose(0, 2, 1), jnp.uint32)


def _dup_heads(w):
    d, width = w.shape
    w = w.reshape(d, width // HEAD_DIM, 1, HEAD_DIM)
    return jnp.broadcast_to(w, (d, width // HEAD_DIM, 2, HEAD_DIM)).reshape(d, 2 * width)


def kernel(x, c, ctx, c_ctx, w_mod, b_mod, norm1_g, norm2_g, w_in, conv_dw, conv_dw_b, conv_ln_g, conv_ln_b,
           attn_sink, pool_w, pool_scale, w_out, w_ffn_in, w_ffn_out, final_g):
    b, n, d = x.shape
    n_ctx = ctx.shape[1]
    depth = w_mod.shape[0]
    heads = attn_sink.shape[1]
    attn_w = heads * HEAD_DIM
    conv_w = conv_dw.shape[2]
    pool_wd = pool_scale.shape[1]
    kv_w = w_in.shape[2] - attn_w - 2 * conv_w - pool_wd
    kv_w //= 2
    dims = (attn_w, 2 * kv_w, conv_w, pool_wd)

    rows = -(-(b + 1) // SUBLANES) * SUBLANES
    cc = jnp.zeros((rows, d), F32).at[:b].set(c).at[b].set(c_ctx)
    mod_all = _modulation(cc, w_mod, b_mod)

    tables = _rope_tables(n)
    bands, bands_ctx = _pool_bands(n), _pool_bands(n_ctx)
    tm_in = min(512, n)
    tm_mix = min(256, n)
    cx = ctx
    for l in range(depth):
        last = l == depth - 1
        mod = mod_all[l, :b].reshape(b, 6, d)
        mod_c = jnp.broadcast_to(mod_all[l, b].reshape(1, 6, d), (b, 6, d))
        wi = w_in[l]
        wq, wk, wv = wi[:, :attn_w], wi[:, attn_w:attn_w + kv_w], wi[:, attn_w + kv_w:attn_w + 2 * kv_w]
        w_kv = jnp.concatenate([_dup_heads(wk), _dup_heads(wv)], axis=1)
        w_all = jnp.concatenate([wq, w_kv, wi[:, attn_w + 2 * kv_w:]], axis=1).astype(BF16)
        g1 = norm1_g[l].reshape(1, d)
        g2 = norm2_g[l].reshape(1, d)
        cvec = jnp.zeros((SUBLANES, conv_w), F32).at[0].set(conv_dw_b[l]).at[1].set(conv_ln_g[l]) \
            .at[2].set(conv_ln_b[l]).at[3].set(pool_scale[l])
        poolw = jax.scipy.linalg.block_diag(*[pool_w[l, gi] for gi in range(pool_w.shape[1])]).astype(BF16)
        head = (attn_sink[l], conv_dw[l], cvec)
        shared = (poolw, _pack_row_pairs(w_out[l].astype(BF16)), g2,
                  _interleave_gate_up(w_ffn_in[l]).astype(BF16), _pack_row_pairs(w_ffn_out[l].astype(BF16)), final_g.reshape(1, d))

        q, kk, vv, h, p = _inproj(x, mod, g1, w_all, tables, tm=tm_in, kv_only=False, dims=dims)
        if last:
            kc, vc = _inproj(cx, mod_c, g1, w_kv.astype(BF16), None, tm=n_ctx, kv_only=True, dims=dims)
        else:
            qc, kc, vc, hc, pc = _inproj(cx, mod_c, g1, w_all, None, tm=n_ctx, kv_only=False, dims=dims)
        x = _mixer(x, q, kk, vv, kc, vc, h, p, mod, *head, bands, *shared, tm=tm_mix, final_norm=last)
        if not last:
            cx = _mixer(cx, qc, None, None, kc, vc, hc, pc, mod_c, *head, bands_ctx, *shared,
                        tm=n_ctx, final_norm=False)
    return x
```

```python
import functools

import jax
import jax.numpy as jnp
from jax import lax
from jax.experimental import pallas as pl
from jax.experimental.pallas import tpu as pltpu

F32 = jnp.float32
BF16 = jnp.bfloat16

LANES = 128
SUBLANES = 8
MXU_COLS = 256
FFN_CHUNK_TILES = 3
VMEM_LIMIT_BYTES = 56 * 1024 * 1024

GRID_W = 64
HEAD_DIM = 64
ROPE_BASE = 10000.0
WINDOW = 128
Q_BLOCK = 128
SPAN = Q_BLOCK + 2 * WINDOW
CONV_KERNEL = 31
CONV_PAD = CONV_KERNEL // 2
CONV_HALO = 16
POOL_WINDOWS = (2, 4, 8, 16)
POOL_HALO = 16
POOL_SPAN = Q_BLOCK + 2 * POOL_HALO
ROW_CHUNK = 64
EPS = 1e-6
NEG = -1e30
LOG2E = 1.4426950408889634


def _rms_norm(x, g):
    return x * lax.rsqrt(jnp.mean(x * x, axis=-1, keepdims=True) + EPS) * g


_dot = functools.partial(jnp.dot, preferred_element_type=F32)
_dot_nt = functools.partial(lax.dot_general, dimension_numbers=(((1,), (1,)), ((), ())), preferred_element_type=F32)


def _mod_kernel(c_ref, w_ref, b_ref, o_ref):
    c = c_ref[...]
    a = c * jax.nn.sigmoid(c)
    o_ref[0] = jnp.dot(a, w_ref[0], preferred_element_type=F32,
                       precision=lax.Precision.HIGHEST) + b_ref[0]


def _modulation(cc, w_mod, b_mod):
    depth, d, d6 = w_mod.shape
    rows = cc.shape[0]
    tn = d6 // 4
    return pl.pallas_call(
        _mod_kernel,
        grid=(depth, d6 // tn),
        in_specs=[
            pl.BlockSpec((rows, d), lambda l, j: (0, 0)),
            pl.BlockSpec((1, d, tn), lambda l, j: (l, 0, j)),
            pl.BlockSpec((1, 1, tn), lambda l, j: (l, 0, j)),
        ],
        out_specs=pl.BlockSpec((1, rows, tn), lambda l, j: (l, 0, j)),
        out_shape=jax.ShapeDtypeStruct((depth, rows, d6), F32),
        compiler_params=pltpu.CompilerParams(
            dimension_semantics=("arbitrary", "arbitrary"), vmem_limit_bytes=VMEM_LIMIT_BYTES),
        name="modulation",
    )(cc, w_mod, b_mod.reshape(depth, 1, d6))


def _rope(t, cos, sin_a, sin_b):
    quarter = HEAD_DIM // 4
    return (t * cos + pltpu.roll(t, LANES - quarter, 1) * sin_a + pltpu.roll(t, quarter, 1) * sin_b)


def _inproj_kernel(*refs, rope, kv_only, attn_w, kv_w, conv_w):
    if rope:
        x_ref, mod_ref, g_ref, w_ref, cos_ref, sa_ref, sb_ref = refs[:7]
        outs = refs[7:]
    else:
        x_ref, mod_ref, g_ref, w_ref = refs[:4]
        outs = refs[4:]
    x = x_ref[0]
    hl = _rms_norm(x, g_ref[...] * (1.0 + mod_ref[0, 1:2, :])) + mod_ref[0, 0:1, :]
    u = _dot(hl.astype(BF16), w_ref[...])

    def maybe_rope(t):
        if rope:
            return _rope(t, cos_ref[...], sa_ref[...], sb_ref[...])
        return t

    if kv_only:
        kk_ref, vv_ref = outs
        off = 0
    else:
        q_ref, kk_ref, vv_ref, h_ref, p_ref = outs
        for c in range(attn_w // LANES):
            t = maybe_rope(u[:, c * LANES:(c + 1) * LANES])
            q_ref[0, :, c * LANES:(c + 1) * LANES] = (t * (HEAD_DIM ** -0.5 * LOG2E)).astype(BF16)
        off = attn_w
    low = lax.broadcasted_iota(jnp.int32, (1, LANES), 1) < HEAD_DIM
    high = jnp.logical_not(low)
    for c in range(kv_w // LANES):
        k = maybe_rope(u[:, off + c * LANES: off + (c + 1) * LANES])
        v = u[:, off + kv_w + c * LANES: off + kv_w + (c + 1) * LANES]
        for t, t_ref in ((k, kk_ref), (v, vv_ref)):
            swapped = pltpu.roll(t, HEAD_DIM, 1)
            for j, (keep, src) in enumerate(((low, t), (high, swapped), (low, swapped), (high, t))):
                cols = slice((4 * c + j) * LANES, (4 * c + j + 1) * LANES)
                t_ref[0, :, cols] = jnp.where(keep, src, 0.0).astype(BF16)
    off += 2 * kv_w
    if not kv_only:
        a = u[:, off:off + conv_w]
        g = u[:, off + conv_w:off + 2 * conv_w]
        h_ref[0] = a * jax.nn.sigmoid(g)
        off += 2 * conv_w
        p_ref[0] = u[:, off:].astype(BF16)


def _inproj(x, mod, g, w, tables, *, tm, kv_only, dims):
    b, n, d = x.shape
    attn_w, kv_w, conv_w, pool_w = dims
    rope = tables is not None
    grid = (n // tm, b)
    in_specs = [
        pl.BlockSpec((1, tm, d), lambda i, bb: (bb, i, 0)),
        pl.BlockSpec((1, 6, d), lambda i, bb: (bb, 0, 0)),
        pl.BlockSpec((1, d), lambda i, bb: (0, 0)),
        pl.BlockSpec(w.shape, lambda i, bb: (0, 0)),
    ]
    args = [x, mod, g, w]
    if rope:
        in_specs += [pl.BlockSpec((tm, LANES), lambda i, bb: (i, 0))] * 3
        args += list(tables)
    tok = lambda width: pl.BlockSpec((1, tm, width), lambda i, bb: (bb, i, 0))
    out_specs = [tok(4 * kv_w), tok(4 * kv_w)]
    out_shape = [jax.ShapeDtypeStruct((b, n, 4 * kv_w), BF16)] * 2
    if not kv_only:
        out_specs = [tok(attn_w)] + out_specs + [tok(conv_w), tok(pool_w)]
        out_shape = ([jax.ShapeDtypeStruct((b, n, attn_w), BF16)] + out_shape
                     + [jax.ShapeDtypeStruct((b, n, conv_w), F32), jax.ShapeDtypeStruct((b, n, pool_w), BF16)])
    kern = functools.partial(_inproj_kernel, rope=rope, kv_only=kv_only,
                             attn_w=attn_w, kv_w=kv_w, conv_w=conv_w)
    return pl.pallas_call(
        kern, grid=grid, in_specs=in_specs, out_specs=out_specs, out_shape=out_shape,
        compiler_params=pltpu.CompilerParams(
            dimension_semantics=("arbitrary", "arbitrary"), vmem_limit_bytes=VMEM_LIMIT_BYTES),
        name="inproj_kv" if kv_only else "inproj",
    )(*args)


def _ffn_tile(d_ff):
    return MXU_COLS if d_ff % MXU_COLS == 0 else LANES


def _ffn_chunks(d_ff):
    unit = _ffn_tile(d_ff)
    units = d_ff // unit
    per = FFN_CHUNK_TILES
    return [(u * unit, min(per, units - u) * unit) for u in range(0, units, per)]


def _order_token(v):
    return (v[0:2 * SUBLANES, 0:LANES] > 0).astype(F32) * 0.0


def _attention_scores(q2, keys, lmask):
    scores = []
    for kcm, klm in keys:
        s_c = _dot_nt(q2, kcm)
        s_l = None if klm is None else jnp.where(lmask, _dot_nt(q2, klm), NEG)
        scores.append((s_c, s_l))
    return scores


def _attention_output(scores, values, sinks):
    out = None
    for (s_c, s_l), (vcm, vlm), sink in zip(scores, values, sinks):
        mx = jnp.maximum(jnp.max(s_c, axis=-1, keepdims=True), sink)
        if s_l is not None:
            mx = jnp.maximum(mx, jnp.max(s_l, axis=-1, keepdims=True))
        e_c = jnp.exp2(s_c - mx)
        den = jnp.exp2(sink - mx) + jnp.sum(e_c, axis=-1, keepdims=True)
        o = _dot(e_c.astype(BF16), vcm)
        if s_l is not None:
            e_l = jnp.exp2(s_l - mx)
            den = den + jnp.sum(e_l, axis=-1, keepdims=True)
            o = o + _dot(e_l.astype(BF16), vlm)
        o = o * (1.0 / den)
        out = o if out is None else out + o
    return out


def _mixer_kernel(*refs, n, tm, has_local, final_norm, attn_w, conv_w, pool_w, d_ff):
    it = iter(refs)
    x_ref, q_ref = next(it), next(it)
    if has_local:
        kk_ref, vv_ref = next(it), next(it)
    kc_ref, vc_ref = next(it), next(it)
    h_ref, p_ref, mod_ref, sink_ref = (next(it) for _ in range(4))
    dw_ref, cvec_ref, band_ref, poolw_ref, wout_ref, g2_ref, wfi_ref, wfo_ref, gf_ref = (next(it) for _ in range(9))
    out_ref = next(it)
    mix_ref, hp_ref, hs_ref, y_ref, y2_ref = (next(it) for _ in range(5))

    s = pl.program_id(0)
    tiles = pl.num_programs(0) - 1
    tiles_per_seq = n // tm
    i = lax.rem(jnp.minimum(s, tiles - 1), tiles_per_seq)
    last_i = tiles_per_seq - 1
    t0 = pl.multiple_of(i * tm, tm)

    @pl.when(s == 0)
    def _():
        mix_ref[...] = jnp.zeros_like(mix_ref)

    ffn_state = {}

    def out_project():
        ffn_state["o"] = _dot(mix_ref[...], wout_ref[...])

    def ffn_norm():
        x1 = x_ref[0] + mod_ref[0, 2:3, :] * ffn_state["o"]
        y2 = _rms_norm(x1, g2_ref[...] * (1.0 + mod_ref[0, 4:5, :])) + mod_ref[0, 3:4, :]
        y2_ref[...] = y2.astype(BF16)
        ffn_state["x1"] = x1

    ffn_parts = []
    deferred = []
    tokens = []

    def ffn_stages(a, w):
        state = {}

        def hidden():
            if tokens:
                corner = (slice(0, 2 * SUBLANES), slice(0, LANES))
                y2_ref[corner] = y2_ref[corner] + sum(tokens).astype(BF16)
                tokens.clear()
            lhs = y2_ref[...]
            state["gate"] = _dot(lhs, wfi_ref[:, a:a + w])
            state["up"] = _dot(lhs, wfi_ref[:, d_ff + a:d_ff + a + w])

        def project():
            gate = state["gate"]
            act = (gate * jax.nn.sigmoid(gate) * state["up"]).astype(BF16)
            part = _dot(act, wfo_ref[a:a + w, :])
            ffn_parts[:] = [part if not ffn_parts else ffn_parts[0] + part]
        return hidden, project

    ffn_pieces = [ffn_stages(a, w) for a, w in _ffn_chunks(d_ff)]


    zero = jnp.zeros((), BF16)
    n_groups = kc_ref.shape[2] // (2 * LANES)
    heads_per_group = attn_w // HEAD_DIM // n_groups

    def attention_stages(blk, g, pr):
        state = {}
        rows = slice(blk * Q_BLOCK, (blk + 1) * Q_BLOCK)
        col = (g * heads_per_group // 2 + pr) * LANES
        head = g * heads_per_group + 2 * pr

        half_cols = [slice((2 * g + hh) * LANES, (2 * g + hh + 1) * LANES) for hh in range(2)]
        start = t0 + blk * Q_BLOCK
        s0 = pl.multiple_of(jnp.clip(start - WINDOW, 0, n - SPAN), Q_BLOCK) if has_local else None

        def scores():
            if has_local:
                r = lax.broadcasted_iota(jnp.int32, (Q_BLOCK, SPAN), 0)
                c = lax.broadcasted_iota(jnp.int32, (Q_BLOCK, SPAN), 1)
                dist = (start - s0) + r - c
                lmask = (dist >= -WINDOW) & (dist <= WINDOW)
            else:
                lmask = None
            keys = [(kc_ref[0, :, cs], kk_ref[0, pl.ds(s0, SPAN), cs] if has_local else None) for cs in half_cols]
            state["scores"] = _attention_scores(q_ref[0, rows, col:col + LANES], keys, lmask)

        def output():
            values = [(vc_ref[0, :, cs], vv_ref[0, pl.ds(s0, SPAN), cs] if has_local else None) for cs in half_cols]
            out = _attention_output(state["scores"], values, (sink_ref[head] * LOG2E, sink_ref[head + 1] * LOG2E))
            deferred.append((mix_ref, (rows, slice(col, col + LANES)), out.astype(BF16)))
        return scores, output

    attention_pieces = [attention_stages(blk, g, pr) for blk in range(tm // Q_BLOCK)
                        for g in range(n_groups) for pr in range(heads_per_group // 2)]

    dw_b, ln_g, ln_b, pool_scale = (cvec_ref[k:k + 1, :] for k in range(4))

    def conv_prepare():
        hp_ref[CONV_HALO:CONV_HALO + tm, :] = h_ref[0, pl.ds(t0, tm), :]
        prev = h_ref[0, pl.ds(pl.multiple_of(jnp.maximum(t0 - CONV_HALO, 0), CONV_HALO), CONV_HALO), :]
        hp_ref[0:CONV_HALO, :] = jnp.where(i > 0, prev, 0.0)
        nxt = h_ref[0, pl.ds(pl.multiple_of(jnp.minimum(t0 + tm, n - CONV_HALO), CONV_HALO), CONV_HALO), :]
        hp_ref[CONV_HALO + tm:, :] = jnp.where(i < last_i, nxt, 0.0)
        shifted_rows = tm + 2 * CONV_HALO - SUBLANES
        for sh in range(1, SUBLANES):
            hs_ref[sh - 1] = hp_ref[sh:sh + shifted_rows, :]

    def conv_piece(r0):
        def run():
            acc = jnp.zeros((ROW_CHUNK, conv_w), F32)
            for k in range(CONV_KERNEL):
                base = r0 + CONV_HALO - CONV_PAD + k
                sh = base % SUBLANES
                if sh == 0:
                    tap = hp_ref[base:base + ROW_CHUNK, :]
                else:
                    tap = hs_ref[sh - 1, base - sh:base - sh + ROW_CHUNK, :]
                acc = acc + tap * dw_ref[k:k + 1, :]
            hc = acc + dw_b
            mu = jnp.mean(hc, axis=-1, keepdims=True)
            cen = hc - mu
            var = jnp.mean(cen * cen, axis=-1, keepdims=True)
            hn = cen * lax.rsqrt(var + EPS) * ln_g + ln_b
            res = hn * jax.nn.sigmoid(hn)
            deferred.append((mix_ref, (slice(r0, r0 + ROW_CHUNK), slice(attn_w, attn_w + conv_w)), res.astype(BF16)))
            tokens.append(_order_token(res))
        return run

    group_w = pool_w // len(POOL_WINDOWS)

    def pool_block(blk):
        def run():
            start = t0 + blk * Q_BLOCK
            s0 = pl.multiple_of(jnp.clip(start - POOL_HALO, 0, n - POOL_SPAN), POOL_HALO)
            which = (start - s0) // POOL_HALO
            span = p_ref[0, pl.ds(s0, POOL_SPAN), :]
            lane = lax.broadcasted_iota(jnp.int32, (1, pool_w), 1)
            y = None
            for gi in range(len(POOL_WINDOWS)):
                in_group = (lane >= gi * group_w) & (lane < (gi + 1) * group_w)
                part = _dot(band_ref[which, gi], jnp.where(in_group, span, zero))
                y = part if y is None else y + part
            y_ref[blk * Q_BLOCK:(blk + 1) * Q_BLOCK, :] = y.astype(BF16)
        return run

    def pool_project():
        pooled = _dot(y_ref[...], poolw_ref[...]) * pool_scale
        deferred.append((mix_ref, (slice(None), slice(attn_w + conv_w, None)), pooled.astype(BF16)))

    n_chunks = len(ffn_pieces)
    per_group = -(-len(attention_pieces) // n_chunks)
    groups = [attention_pieces[j * per_group:(j + 1) * per_group] for j in range(n_chunks)]
    score_stage = lambda j: [scores for scores, _ in groups[j]]
    output_stage = lambda j: [output for _, output in groups[j]]
    hidden_stage = lambda j: [ffn_pieces[j][0]]
    project_stage = lambda j: [ffn_pieces[j][1]]
    elementwise = ([conv_prepare] + [conv_piece(r0) for r0 in range(0, tm, ROW_CHUNK)]
                   + [pool_block(blk) for blk in range(tm // Q_BLOCK)])
    per_elem = -(-len(elementwise) // max(n_chunks - 1, 1))
    order = [out_project, ffn_norm] + score_stage(0)
    for j in range(n_chunks):
        order += hidden_stage(j) + elementwise[j * per_elem:(j + 1) * per_elem] + output_stage(j)
        if j + 1 < n_chunks:
            order += score_stage(j + 1)
        order += project_stage(j)
    order += [pool_project]
    for stage in order:
        stage()

    x2 = ffn_state["x1"] + mod_ref[0, 5:6, :] * ffn_parts[0]
    if final_norm:
        x2 = _rms_norm(x2, gf_ref[...])
    out_ref[0] = x2
    for ref, index, value in deferred:
        ref[index] = value


def _mixer(x, q, kk, vv, kc, vc, h, p, mod, sink, dw, cvec, band, poolw, wout, g2, wfi, wfo, gf,
           *, tm, final_norm):
    b, n, d = x.shape
    has_local = kk is not None
    attn_w, kv_w, conv_w, pool_w = q.shape[2], kc.shape[2], h.shape[2], p.shape[2]
    n_ctx = kc.shape[1]
    d_ff = wfo.shape[0]
    tiles_per_seq = n // tm
    tiles = b * tiles_per_seq
    mix_tile = lambda s: jnp.minimum(s, tiles - 1)
    ffn_tile = lambda s: jnp.maximum(s - 1, 0)
    const = lambda shape: pl.BlockSpec(shape, lambda s: (0,) * len(shape), pipeline_mode=pl.Buffered(1))
    tok = lambda width, tile: pl.BlockSpec(
        (1, tm, width), lambda s: (tile(s) // tiles_per_seq, tile(s) % tiles_per_seq, 0))
    seq = lambda length, width, tile: pl.BlockSpec((1, length, width), lambda s: (tile(s) // tiles_per_seq, 0, 0))
    in_specs = [tok(d, ffn_tile), tok(attn_w, mix_tile)]
    args = [x, q]
    if has_local:
        in_specs += [seq(n, kv_w, mix_tile), seq(n, kv_w, mix_tile)]
        args += [kk, vv]
    in_specs += [seq(n_ctx, kv_w, mix_tile), seq(n_ctx, kv_w, mix_tile), seq(n, conv_w, mix_tile),
                 seq(n, pool_w, mix_tile), seq(6, d, ffn_tile),
                 pl.BlockSpec(memory_space=pltpu.SMEM),
                 const(dw.shape), const(cvec.shape), const(band.shape), const(poolw.shape), const(wout.shape),
                 const(g2.shape), const(wfi.shape), const(wfo.shape), const(gf.shape)]
    args += [kc, vc, h, p, mod, sink, dw, cvec, band, poolw, wout, g2, wfi, wfo, gf]
    kern = functools.partial(_mixer_kernel, n=n, tm=tm, has_local=has_local, final_norm=final_norm,
                             attn_w=attn_w, conv_w=conv_w, pool_w=pool_w, d_ff=d_ff)
    return pl.pallas_call(
        kern, grid=(tiles + 1,), in_specs=in_specs, out_specs=tok(d, ffn_tile),
        out_shape=jax.ShapeDtypeStruct((b, n, d), F32),
        scratch_shapes=[
            pltpu.VMEM((tm, attn_w + conv_w + pool_w), BF16),
            pltpu.VMEM((tm + 2 * CONV_HALO, conv_w), F32),
            pltpu.VMEM((SUBLANES - 1, tm + 2 * CONV_HALO - SUBLANES, conv_w), F32),
            pltpu.VMEM((tm, pool_w), BF16),
            pltpu.VMEM((tm, d), BF16),
        ],
        compiler_params=pltpu.CompilerParams(
            dimension_semantics=("arbitrary",), vmem_limit_bytes=VMEM_LIMIT_BYTES),
        name="mixer_local" if has_local else "mixer_ctx",
    )(*args)


def _rope_tables(n):
    rows = n // GRID_W
    row = jnp.repeat(jnp.arange(rows), GRID_W).astype(F32)
    col = jnp.tile(jnp.arange(GRID_W), rows).astype(F32)
    half = HEAD_DIM // 2
    inv = ROPE_BASE ** (-jnp.arange(0, half, 2, dtype=F32) / half)
    ar = row[:, None] * inv
    ac = col[:, None] * inv
    ang = jnp.concatenate([ar, ar, ac, ac], axis=-1)
    cos, sin = jnp.cos(ang), jnp.sin(ang)
    first = (jnp.arange(HEAD_DIM) // (HEAD_DIM // 4)) % 2 == 0
    sin_a = jnp.where(first, -sin, 0.0)
    sin_b = jnp.where(first, 0.0, sin)
    rep = LANES // HEAD_DIM
    return tuple(jnp.tile(t, (1, rep)) for t in (cos, sin_a, sin_b))


def _pool_bands(n):
    assert n >= POOL_SPAN and n % Q_BLOCK == 0
    r = jnp.arange(Q_BLOCK)[:, None]
    pos = jnp.arange(POOL_SPAN)[None, :]
    variants = []
    for k in range(3):
        t = k * POOL_HALO + r
        first = 0 if k == 0 else -POOL_SPAN
        last = POOL_SPAN - 1 if k == 2 else 2 * POOL_SPAN
        per_window = []
        for win in POOL_WINDOWS:
            lo = jnp.maximum(t - win // 2, first)
            hi = jnp.minimum(t + win - 1 - win // 2, last)
            inside = (pos >= lo) & (pos <= hi)
            mean = jnp.where(inside, 1.0 / (hi - lo + 1).astype(F32), 0.0)
            per_window.append(mean - (pos == t).astype(F32))
        variants.append(jnp.stack(per_window))
    return jnp.stack(variants).astype(BF16)


def kernel(x, c, ctx, c_ctx, w_mod, b_mod, norm1_g, norm2_g, w_in, conv_dw, conv_dw_b, conv_ln_g, conv_ln_b,
           attn_sink, pool_w, pool_scale, w_out, w_ffn_in, w_ffn_out, final_g):
    b, n, d = x.shape
    n_ctx = ctx.shape[1]
    depth = w_mod.shape[0]
    heads = attn_sink.shape[1]
    attn_w = heads * HEAD_DIM
    conv_w = conv_dw.shape[2]
    pool_wd = pool_scale.shape[1]
    kv_w = w_in.shape[2] - attn_w - 2 * conv_w - pool_wd
    kv_w //= 2
    assert kv_w % LANES == 0
    dims = (attn_w, kv_w, conv_w, pool_wd)

    rows = -(-(b + 1) // SUBLANES) * SUBLANES
    cc = jnp.zeros((rows, d), F32).at[:b].set(c).at[b].set(c_ctx)
    mod_all = _modulation(cc, w_mod, b_mod)

    tables = _rope_tables(n)
    bands, bands_ctx = _pool_bands(n), _pool_bands(n_ctx)
    tm_in = min(512, n)
    tm_mix = min(256, n)
    cx = ctx
    for l in range(depth):
        last = l == depth - 1
        mod = mod_all[l, :b].reshape(b, 6, d)
        mod_c = jnp.broadcast_to(mod_all[l, b].reshape(1, 6, d), (b, 6, d))
        w_all = w_in[l].astype(BF16)
        w_kv = w_all[:, attn_w:attn_w + 2 * kv_w]
        g1 = norm1_g[l].reshape(1, d)
        g2 = norm2_g[l].reshape(1, d)
        cvec = jnp.zeros((SUBLANES, conv_w), F32).at[0].set(conv_dw_b[l]).at[1].set(conv_ln_g[l]) \
            .at[2].set(conv_ln_b[l]).at[3].set(pool_scale[l])
        poolw = jax.scipy.linalg.block_diag(*[pool_w[l, gi] for gi in range(pool_w.shape[1])]).astype(BF16)
        head = (attn_sink[l], conv_dw[l], cvec)
        shared = (poolw, w_out[l].astype(BF16), g2,
                  w_ffn_in[l].astype(BF16), w_ffn_out[l].astype(BF16), final_g.reshape(1, d))

        q, kk, vv, h, p = _inproj(x, mod, g1, w_all, tables, tm=tm_in, kv_only=False, dims=dims)
        if last:
            kc, vc = _inproj(cx, mod_c, g1, w_kv, None, tm=n_ctx, kv_only=True, dims=dims)
        else:
            qc, kc, vc, hc, pc = _inproj(cx, mod_c, g1, w_all, None, tm=n_ctx, kv_only=False, dims=dims)
        x = _mixer(x, q, kk, vv, kc, vc, h, p, mod, *head, bands, *shared, tm=tm_mix, final_norm=last)
        if not last:
            cx = _mixer(cx, qc, None, None, kc, vc, hc, pc, mod_c, *head, bands_ctx, *shared,
                        tm=n_ctx, final_norm=False)
    return x
```

```python
import functools

import jax
import jax.numpy as jnp
from jax import lax
from jax.experimental import pallas as pl
from jax.experimental.pallas import tpu as pltpu

F32 = jnp.float32
BF16 = jnp.bfloat16

LANES = 128
SUBLANES = 8
MXU_COLS = 256
FFN_CHUNK_TILES = 3
VMEM_LIMIT_BYTES = 56 * 1024 * 1024

GRID_W = 64
HEAD_DIM = 64
ROPE_BASE = 10000.0
WINDOW = 128
Q_BLOCK = 128
SPAN = Q_BLOCK + 2 * WINDOW
CONV_KERNEL = 31
CONV_PAD = CONV_KERNEL // 2
CONV_HALO = 16
POOL_WINDOWS = (2, 4, 8, 16)
POOL_HALO = 16
POOL_SPAN = Q_BLOCK + 2 * POOL_HALO
ROW_CHUNK = 64
EPS = 1e-6
NEG = -1e30
LOG2E = 1.4426950408889634


def _rms_norm(x, g):
    return x * lax.rsqrt(jnp.mean(x * x, axis=-1, keepdims=True) + EPS) * g


_dot = functools.partial(jnp.dot, preferred_element_type=F32)
_dot_nt = functools.partial(lax.dot_general, dimension_numbers=(((1,), (1,)), ((), ())), preferred_element_type=F32)


def _mod_kernel(c_ref, w_ref, b_ref, o_ref):
    c = c_ref[...]
    a = c * jax.nn.sigmoid(c)
    o_ref[0] = jnp.dot(a, w_ref[0], preferred_element_type=F32,
                       precision=lax.Precision.HIGHEST) + b_ref[0]


def _modulation(cc, w_mod, b_mod):
    depth, d, d6 = w_mod.shape
    rows = cc.shape[0]
    tn = d6 // 4
    return pl.pallas_call(
        _mod_kernel,
        grid=(depth, d6 // tn),
        in_specs=[
            pl.BlockSpec((rows, d), lambda l, j: (0, 0)),
            pl.BlockSpec((1, d, tn), lambda l, j: (l, 0, j)),
            pl.BlockSpec((1, 1, tn), lambda l, j: (l, 0, j)),
        ],
        out_specs=pl.BlockSpec((1, rows, tn), lambda l, j: (l, 0, j)),
        out_shape=jax.ShapeDtypeStruct((depth, rows, d6), F32),
        compiler_params=pltpu.CompilerParams(
            dimension_semantics=("arbitrary", "arbitrary"), vmem_limit_bytes=VMEM_LIMIT_BYTES),
        name="modulation",
    )(cc, w_mod, b_mod.reshape(depth, 1, d6))


def _rope(t, cos, sin_a, sin_b):
    quarter = HEAD_DIM // 4
    return (t * cos + pltpu.roll(t, LANES - quarter, 1) * sin_a + pltpu.roll(t, quarter, 1) * sin_b)


def _inproj_kernel(*refs, rope, kv_only, attn_w, kv_w, conv_w):
    if rope:
        x_ref, mod_ref, g_ref, w_ref, cos_ref, sa_ref, sb_ref = refs[:7]
        outs = refs[7:]
    else:
        x_ref, mod_ref, g_ref, w_ref = refs[:4]
        outs = refs[4:]
    x = x_ref[0]
    hl = _rms_norm(x, g_ref[...] * (1.0 + mod_ref[0, 1:2, :])) + mod_ref[0, 0:1, :]
    u = _dot(hl.astype(BF16), w_ref[...])

    def maybe_rope(t):
        if rope:
            return _rope(t, cos_ref[...], sa_ref[...], sb_ref[...])
        return t

    if kv_only:
        kk_ref, vv_ref = outs
        off = 0
    else:
        q_ref, kk_ref, vv_ref, h_ref, p_ref = outs
        for c in range(attn_w // LANES):
            t = maybe_rope(u[:, c * LANES:(c + 1) * LANES])
            q_ref[0, :, c * LANES:(c + 1) * LANES] = (t * (HEAD_DIM ** -0.5 * LOG2E)).astype(BF16)
        off = attn_w
    low = lax.broadcasted_iota(jnp.int32, (1, LANES), 1) < HEAD_DIM
    high = jnp.logical_not(low)
    for c in range(kv_w // LANES):
        k = maybe_rope(u[:, off + c * LANES: off + (c + 1) * LANES])
        v = u[:, off + kv_w + c * LANES: off + kv_w + (c + 1) * LANES]
        for t, t_ref in ((k, kk_ref), (v, vv_ref)):
            swapped = pltpu.roll(t, HEAD_DIM, 1)
            for j, (keep, src) in enumerate(((low, t), (high, swapped), (low, swapped), (high, t))):
                cols = slice((4 * c + j) * LANES, (4 * c + j + 1) * LANES)
                t_ref[0, :, cols] = jnp.where(keep, src, 0.0).astype(BF16)
    off += 2 * kv_w
    if not kv_only:
        a = u[:, off:off + conv_w]
        g = u[:, off + conv_w:off + 2 * conv_w]
        h_ref[0] = a * jax.nn.sigmoid(g)
        off += 2 * conv_w
        p_ref[0] = u[:, off:].astype(BF16)


def _inproj(x, mod, g, w, tables, *, tm, kv_only, dims):
    b, n, d = x.shape
    attn_w, kv_w, conv_w, pool_w = dims
    rope = tables is not None
    grid = (n // tm, b)
    in_specs = [
        pl.BlockSpec((1, tm, d), lambda i, bb: (bb, i, 0)),
        pl.BlockSpec((1, 6, d), lambda i, bb: (bb, 0, 0)),
        pl.BlockSpec((1, d), lambda i, bb: (0, 0)),
        pl.BlockSpec(w.shape, lambda i, bb: (0, 0)),
    ]
    args = [x, mod, g, w]
    if rope:
        in_specs += [pl.BlockSpec((tm, LANES), lambda i, bb: (i, 0))] * 3
        args += list(tables)
    tok = lambda width: pl.BlockSpec((1, tm, width), lambda i, bb: (bb, i, 0))
    out_specs = [tok(4 * kv_w), tok(4 * kv_w)]
    out_shape = [jax.ShapeDtypeStruct((b, n, 4 * kv_w), BF16)] * 2
    if not kv_only:
        out_specs = [tok(attn_w)] + out_specs + [tok(conv_w), tok(pool_w)]
        out_shape = ([jax.ShapeDtypeStruct((b, n, attn_w), BF16)] + out_shape
                     + [jax.ShapeDtypeStruct((b, n, conv_w), F32), jax.ShapeDtypeStruct((b, n, pool_w), BF16)])
    kern = functools.partial(_inproj_kernel, rope=rope, kv_only=kv_only,
                             attn_w=attn_w, kv_w=kv_w, conv_w=conv_w)
    return pl.pallas_call(
        kern, grid=grid, in_specs=in_specs, out_specs=out_specs, out_shape=out_shape,
        compiler_params=pltpu.CompilerParams(
            dimension_semantics=("arbitrary", "arbitrary"), vmem_limit_bytes=VMEM_LIMIT_BYTES),
        name="inproj_kv" if kv_only else "inproj",
    )(*args)


def _ffn_tile(d_ff):
    return MXU_COLS if d_ff % MXU_COLS == 0 else LANES


def _ffn_chunks(d_ff):
    unit = _ffn_tile(d_ff)
    units = d_ff // unit
    per = FFN_CHUNK_TILES
    return [(u * unit, min(per, units - u) * unit) for u in range(0, units, per)]


def _order_token(v):
    return (v[0:2 * SUBLANES, 0:LANES] > 0).astype(F32) * 0.0


def _attention_scores(q2, keys, lmask):
    scores = []
    for kcm, klm in keys:
        s_c = _dot_nt(q2, kcm)
        s_l = None if klm is None else jnp.where(lmask, _dot_nt(q2, klm), NEG)
        scores.append((s_c, s_l))
    return scores


def _attention_output(scores, values, sinks):
    out = None
    for (s_c, s_l), (vcm, vlm), sink in zip(scores, values, sinks):
        mx = jnp.maximum(jnp.max(s_c, axis=-1, keepdims=True), sink)
        if s_l is not None:
            mx = jnp.maximum(mx, jnp.max(s_l, axis=-1, keepdims=True))
        e_c = jnp.exp2(s_c - mx)
        den = jnp.exp2(sink - mx) + jnp.sum(e_c, axis=-1, keepdims=True)
        o = _dot(e_c.astype(BF16), vcm)
        if s_l is not None:
            e_l = jnp.exp2(s_l - mx)
            den = den + jnp.sum(e_l, axis=-1, keepdims=True)
            o = o + _dot(e_l.astype(BF16), vlm)
        o = o * (1.0 / den)
        out = o if out is None else out + o
    return out


def _mixer_kernel(*refs, n, tm, has_local, final_norm, attn_w, conv_w, pool_w, d_ff):
    it = iter(refs)
    x_ref, q_ref = next(it), next(it)
    if has_local:
        kk_ref, vv_ref = next(it), next(it)
    kc_ref, vc_ref = next(it), next(it)
    h_ref, p_ref, mod_ref, sink_ref = (next(it) for _ in range(4))
    dw_ref, cvec_ref, band_ref, poolw_ref, wout_ref, g2_ref, wfi_ref, wfo_ref, gf_ref = (next(it) for _ in range(9))
    out_ref = next(it)
    mix_ref, hp_ref, hs_ref, y_ref, y2_ref = (next(it) for _ in range(5))

    s = pl.program_id(0)
    tiles = pl.num_programs(0) - 1
    tiles_per_seq = n // tm
    i = lax.rem(jnp.minimum(s, tiles - 1), tiles_per_seq)
    last_i = tiles_per_seq - 1
    t0 = pl.multiple_of(i * tm, tm)

    @pl.when(s == 0)
    def _():
        mix_ref[...] = jnp.zeros_like(mix_ref)

    ffn_state = {}

    def out_project():
        ffn_state["o"] = _dot(mix_ref[...], wout_ref[:, 0:x_ref.shape[2]])

    def ffn_norm():
        x1 = x_ref[0] + mod_ref[0, 2:3, :] * ffn_state["o"]
        y2 = _rms_norm(x1, g2_ref[...] * (1.0 + mod_ref[0, 4:5, :])) + mod_ref[0, 3:4, :]
        y2_ref[...] = y2.astype(BF16)
        ffn_state["x1"] = x1

    ffn_parts = []
    deferred = []
    tokens = []

    def ffn_stages(a, w):
        state = {}

        def hidden():
            if tokens:
                corner = (slice(0, 2 * SUBLANES), slice(0, LANES))
                y2_ref[corner] = y2_ref[corner] + sum(tokens).astype(BF16)
                tokens.clear()
            lhs = y2_ref[...]
            state["gate"] = _dot(lhs, wfi_ref[:, a:a + w])
            state["up"] = _dot(lhs, wfi_ref[:, d_ff + a:d_ff + a + w])

        def project():
            gate = state["gate"]
            act = (gate * jax.nn.sigmoid(gate) * state["up"]).astype(BF16)
            part = _dot(act, wfo_ref[a:a + w, 0:x_ref.shape[2]])
            ffn_parts[:] = [part if not ffn_parts else ffn_parts[0] + part]
        return hidden, project

    ffn_pieces = [ffn_stages(a, w) for a, w in _ffn_chunks(d_ff)]


    zero = jnp.zeros((), BF16)
    n_groups = kc_ref.shape[2] // (2 * LANES)
    heads_per_group = attn_w // HEAD_DIM // n_groups

    def attention_stages(blk, g, pr):
        state = {}
        rows = slice(blk * Q_BLOCK, (blk + 1) * Q_BLOCK)
        col = (g * heads_per_group // 2 + pr) * LANES
        head = g * heads_per_group + 2 * pr

        half_cols = [slice((2 * g + hh) * LANES, (2 * g + hh + 1) * LANES) for hh in range(2)]
        start = t0 + blk * Q_BLOCK
        s0 = pl.multiple_of(jnp.clip(start - WINDOW, 0, n - SPAN), Q_BLOCK) if has_local else None

        def scores():
            if has_local:
                r = lax.broadcasted_iota(jnp.int32, (Q_BLOCK, SPAN), 0)
                c = lax.broadcasted_iota(jnp.int32, (Q_BLOCK, SPAN), 1)
                dist = (start - s0) + r - c
                lmask = (dist >= -WINDOW) & (dist <= WINDOW)
            else:
                lmask = None
            keys = [(kc_ref[0, :, cs], kk_ref[0, pl.ds(s0, SPAN), cs] if has_local else None) for cs in half_cols]
            state["scores"] = _attention_scores(q_ref[0, rows, col:col + LANES], keys, lmask)

        def output():
            values = [(vc_ref[0, :, cs], vv_ref[0, pl.ds(s0, SPAN), cs] if has_local else None) for cs in half_cols]
            out = _attention_output(state["scores"], values, (sink_ref[head] * LOG2E, sink_ref[head + 1] * LOG2E))
            deferred.append((mix_ref, (rows, slice(col, col + LANES)), out.astype(BF16)))
        return scores, output

    attention_pieces = [attention_stages(blk, g, pr) for blk in range(tm // Q_BLOCK)
                        for g in range(n_groups) for pr in range(heads_per_group // 2)]

    dw_b, ln_g, ln_b, pool_scale = (cvec_ref[k:k + 1, :] for k in range(4))

    def conv_prepare():
        hp_ref[CONV_HALO:CONV_HALO + tm, :] = h_ref[0, pl.ds(t0, tm), :]
        prev = h_ref[0, pl.ds(pl.multiple_of(jnp.maximum(t0 - CONV_HALO, 0), CONV_HALO), CONV_HALO), :]
        hp_ref[0:CONV_HALO, :] = jnp.where(i > 0, prev, 0.0)
        nxt = h_ref[0, pl.ds(pl.multiple_of(jnp.minimum(t0 + tm, n - CONV_HALO), CONV_HALO), CONV_HALO), :]
        hp_ref[CONV_HALO + tm:, :] = jnp.where(i < last_i, nxt, 0.0)
        shifted_rows = tm + 2 * CONV_HALO - SUBLANES
        for sh in range(1, SUBLANES):
            hs_ref[sh - 1] = hp_ref[sh:sh + shifted_rows, :]

    def conv_piece(r0):
        def run():
            acc = jnp.zeros((ROW_CHUNK, conv_w), F32)
            for k in range(CONV_KERNEL):
                base = r0 + CONV_HALO - CONV_PAD + k
                sh = base % SUBLANES
                if sh == 0:
                    tap = hp_ref[base:base + ROW_CHUNK, :]
                else:
                    tap = hs_ref[sh - 1, base - sh:base - sh + ROW_CHUNK, :]
                acc = acc + tap * dw_ref[k:k + 1, :]
            hc = acc + dw_b
            mu = jnp.mean(hc, axis=-1, keepdims=True)
            cen = hc - mu
            var = jnp.mean(cen * cen, axis=-1, keepdims=True)
            hn = cen * lax.rsqrt(var + EPS) * ln_g + ln_b
            res = hn * jax.nn.sigmoid(hn)
            deferred.append((mix_ref, (slice(r0, r0 + ROW_CHUNK), slice(attn_w, attn_w + conv_w)), res.astype(BF16)))
            tokens.append(_order_token(res))
        return run

    group_w = pool_w // len(POOL_WINDOWS)

    def pool_block(blk):
        def run():
            start = t0 + blk * Q_BLOCK
            s0 = pl.multiple_of(jnp.clip(start - POOL_HALO, 0, n - POOL_SPAN), POOL_HALO)
            which = (start - s0) // POOL_HALO
            span = p_ref[0, pl.ds(s0, POOL_SPAN), :]
            lane = lax.broadcasted_iota(jnp.int32, (1, pool_w), 1)
            y = None
            for gi in range(len(POOL_WINDOWS)):
                in_group = (lane >= gi * group_w) & (lane < (gi + 1) * group_w)
                part = _dot(band_ref[which, gi], jnp.where(in_group, span, zero))
                y = part if y is None else y + part
            y_ref[blk * Q_BLOCK:(blk + 1) * Q_BLOCK, :] = y.astype(BF16)
        return run

    def pool_project():
        pooled = _dot(y_ref[...], poolw_ref[...]) * pool_scale
        deferred.append((mix_ref, (slice(None), slice(attn_w + conv_w, None)), pooled.astype(BF16)))

    n_chunks = len(ffn_pieces)
    per_group = -(-len(attention_pieces) // n_chunks)
    groups = [attention_pieces[j * per_group:(j + 1) * per_group] for j in range(n_chunks)]
    score_stage = lambda j: [scores for scores, _ in groups[j]]
    output_stage = lambda j: [output for _, output in groups[j]]
    hidden_stage = lambda j: [ffn_pieces[j][0]]
    project_stage = lambda j: [ffn_pieces[j][1]]
    elementwise = ([conv_prepare] + [conv_piece(r0) for r0 in range(0, tm, ROW_CHUNK)]
                   + [pool_block(blk) for blk in range(tm // Q_BLOCK)])
    per_elem = -(-len(elementwise) // max(n_chunks - 1, 1))
    order = [out_project, ffn_norm] + score_stage(0)
    for j in range(n_chunks):
        order += hidden_stage(j) + elementwise[j * per_elem:(j + 1) * per_elem] + output_stage(j)
        if j + 1 < n_chunks:
            order += score_stage(j + 1)
        order += project_stage(j)
    order += [pool_project]
    for stage in order:
        stage()

    x2 = ffn_state["x1"] + mod_ref[0, 5:6, :] * ffn_parts[0]
    if final_norm:
        x2 = _rms_norm(x2, gf_ref[...])
    out_ref[0] = x2
    for ref, index, value in deferred:
        ref[index] = value


def _mixer(x, q, kk, vv, kc, vc, h, p, mod, sink, dw, cvec, band, poolw, wout, g2, wfi, wfo, gf,
           *, tm, final_norm):
    b, n, d = x.shape
    has_local = kk is not None
    attn_w, kv_w, conv_w, pool_w = q.shape[2], kc.shape[2], h.shape[2], p.shape[2]
    n_ctx = kc.shape[1]
    d_ff = wfo.shape[0]
    tiles_per_seq = n // tm
    tiles = b * tiles_per_seq
    mix_tile = lambda s: jnp.minimum(s, tiles - 1)
    ffn_tile = lambda s: jnp.maximum(s - 1, 0)
    const = lambda shape: pl.BlockSpec(shape, lambda s: (0,) * len(shape), pipeline_mode=pl.Buffered(1))
    tok = lambda width, tile: pl.BlockSpec(
        (1, tm, width), lambda s: (tile(s) // tiles_per_seq, tile(s) % tiles_per_seq, 0))
    seq = lambda length, width, tile: pl.BlockSpec((1, length, width), lambda s: (tile(s) // tiles_per_seq, 0, 0))
    in_specs = [tok(d, ffn_tile), tok(attn_w, mix_tile)]
    args = [x, q]
    if has_local:
        in_specs += [seq(n, kv_w, mix_tile), seq(n, kv_w, mix_tile)]
        args += [kk, vv]
    in_specs += [seq(n_ctx, kv_w, mix_tile), seq(n_ctx, kv_w, mix_tile), seq(n, conv_w, mix_tile),
                 seq(n, pool_w, mix_tile), seq(6, d, ffn_tile),
                 pl.BlockSpec(memory_space=pltpu.SMEM),
                 const(dw.shape), const(cvec.shape), const(band.shape), const(poolw.shape), const(wout.shape),
                 const(g2.shape), const(wfi.shape), const(wfo.shape), const(gf.shape)]
    args += [kc, vc, h, p, mod, sink, dw, cvec, band, poolw, wout, g2, wfi, wfo, gf]
    kern = functools.partial(_mixer_kernel, n=n, tm=tm, has_local=has_local, final_norm=final_norm,
                             attn_w=attn_w, conv_w=conv_w, pool_w=pool_w, d_ff=d_ff)
    return pl.pallas_call(
        kern, grid=(tiles + 1,), in_specs=in_specs, out_specs=tok(d, ffn_tile),
        out_shape=jax.ShapeDtypeStruct((b, n, d), F32),
        scratch_shapes=[
            pltpu.VMEM((tm, attn_w + conv_w + pool_w), BF16),
            pltpu.VMEM((tm + 2 * CONV_HALO, conv_w), F32),
            pltpu.VMEM((SUBLANES - 1, tm + 2 * CONV_HALO - SUBLANES, conv_w), F32),
            pltpu.VMEM((tm, pool_w), BF16),
            pltpu.VMEM((tm, d), BF16),
        ],
        compiler_params=pltpu.CompilerParams(
            dimension_semantics=("arbitrary",), vmem_limit_bytes=VMEM_LIMIT_BYTES),
        name="mixer_local" if has_local else "mixer_ctx",
    )(*args)


def _rope_tables(n):
    rows = n // GRID_W
    row = jnp.repeat(jnp.arange(rows), GRID_W).astype(F32)
    col = jnp.tile(jnp.arange(GRID_W), rows).astype(F32)
    half = HEAD_DIM // 2
    inv = ROPE_BASE ** (-jnp.arange(0, half, 2, dtype=F32) / half)
    ar = row[:, None] * inv
    ac = col[:, None] * inv
    ang = jnp.concatenate([ar, ar, ac, ac], axis=-1)
    cos, sin = jnp.cos(ang), jnp.sin(ang)
    first = (jnp.arange(HEAD_DIM) // (HEAD_DIM // 4)) % 2 == 0
    sin_a = jnp.where(first, -sin, 0.0)
    sin_b = jnp.where(first, 0.0, sin)
    rep = LANES // HEAD_DIM
    return tuple(jnp.tile(t, (1, rep)) for t in (cos, sin_a, sin_b))


def _pool_bands(n):
    assert n >= POOL_SPAN and n % Q_BLOCK == 0
    r = jnp.arange(Q_BLOCK)[:, None]
    pos = jnp.arange(POOL_SPAN)[None, :]
    variants = []
    for k in range(3):
        t = k * POOL_HALO + r
        first = 0 if k == 0 else -POOL_SPAN
        last = POOL_SPAN - 1 if k == 2 else 2 * POOL_SPAN
        per_window = []
        for win in POOL_WINDOWS:
            lo = jnp.maximum(t - win // 2, first)
            hi = jnp.minimum(t + win - 1 - win // 2, last)
            inside = (pos >= lo) & (pos <= hi)
            mean = jnp.where(inside, 1.0 / (hi - lo + 1).astype(F32), 0.0)
            per_window.append(mean - (pos == t).astype(F32))
        variants.append(jnp.stack(per_window))
    return jnp.stack(variants).astype(BF16)


def _pad_lanes(w):
    return jnp.pad(w, ((0, 0), (0, LANES)))


def kernel(x, c, ctx, c_ctx, w_mod, b_mod, norm1_g, norm2_g, w_in, conv_dw, conv_dw_b, conv_ln_g, conv_ln_b,
           attn_sink, pool_w, pool_scale, w_out, w_ffn_in, w_ffn_out, final_g):
    b, n, d = x.shape
    n_ctx = ctx.shape[1]
    depth = w_mod.shape[0]
    heads = attn_sink.shape[1]
    attn_w = heads * HEAD_DIM
    conv_w = conv_dw.shape[2]
    pool_wd = pool_scale.shape[1]
    kv_w = w_in.shape[2] - attn_w - 2 * conv_w - pool_wd
    kv_w //= 2
    assert kv_w % LANES == 0
    dims = (attn_w, kv_w, conv_w, pool_wd)

    rows = -(-(b + 1) // SUBLANES) * SUBLANES
    cc = jnp.zeros((rows, d), F32).at[:b].set(c).at[b].set(c_ctx)
    mod_all = _modulation(cc, w_mod, b_mod)

    tables = _rope_tables(n)
    bands, bands_ctx = _pool_bands(n), _pool_bands(n_ctx)
    tm_in = min(512, n)
    tm_mix = min(256, n)
    cx = ctx
    for l in range(depth):
        last = l == depth - 1
        mod = mod_all[l, :b].reshape(b, 6, d)
        mod_c = jnp.broadcast_to(mod_all[l, b].reshape(1, 6, d), (b, 6, d))
        w_all = w_in[l].astype(BF16)
        w_kv = w_all[:, attn_w:attn_w + 2 * kv_w]
        g1 = norm1_g[l].reshape(1, d)
        g2 = norm2_g[l].reshape(1, d)
        cvec = jnp.zeros((SUBLANES, conv_w), F32).at[0].set(conv_dw_b[l]).at[1].set(conv_ln_g[l]) \
            .at[2].set(conv_ln_b[l]).at[3].set(pool_scale[l])
        poolw = jax.scipy.linalg.block_diag(*[pool_w[l, gi] for gi in range(pool_w.shape[1])]).astype(BF16)
        head = (attn_sink[l], conv_dw[l], cvec)
        shared = (poolw, _pad_lanes(w_out[l].astype(BF16)), g2,
                  w_ffn_in[l].astype(BF16), _pad_lanes(w_ffn_out[l].astype(BF16)), final_g.reshape(1, d))

        q, kk, vv, h, p = _inproj(x, mod, g1, w_all, tables, tm=tm_in, kv_only=False, dims=dims)
        if last:
            kc, vc = _inproj(cx, mod_c, g1, w_kv, None, tm=n_ctx, kv_only=True, dims=dims)
        else:
            qc, kc, vc, hc, pc = _inproj(cx, mod_c, g1, w_all, None, tm=n_ctx, kv_only=False, dims=dims)
        x = _mixer(x, q, kk, vv, kc, vc, h, p, mod, *head, bands, *shared, tm=tm_mix, final_norm=last)
        if not last:
            cx = _mixer(cx, qc, None, None, kc, vc, hc, pc, mod_c, *head, bands_ctx, *shared,
                        tm=n_ctx, final_norm=False)
    return x
```

```python
import functools

import jax
import jax.numpy as jnp
from jax import lax
from jax.experimental import pallas as pl
from jax.experimental.pallas import tpu as pltpu

F32 = jnp.float32
BF16 = jnp.bfloat16

LANES = 128
SUBLANES = 8
MXU_COLS = 256
FFN_CHUNK_TILES = 3
VMEM_LIMIT_BYTES = 56 * 1024 * 1024

GRID_W = 64
HEAD_DIM = 64
ROPE_BASE = 10000.0
WINDOW = 128
Q_BLOCK = 128
SPAN = Q_BLOCK + 2 * WINDOW
CONV_KERNEL = 31
CONV_PAD = CONV_KERNEL // 2
CONV_HALO = 16
POOL_WINDOWS = (2, 4, 8, 16)
POOL_HALO = 16
POOL_SPAN = Q_BLOCK + 2 * POOL_HALO
ROW_CHUNK = 64
INPROJ_ROW_GROUPS = 2
EPS = 1e-6
NEG = -1e30
LOG2E = 1.4426950408889634


def _rms_norm(x, g):
    return x * lax.rsqrt(jnp.mean(x * x, axis=-1, keepdims=True) + EPS) * g


_dot = functools.partial(jnp.dot, preferred_element_type=F32)
_dot_nt = functools.partial(lax.dot_general, dimension_numbers=(((1,), (1,)), ((), ())), preferred_element_type=F32)


def _mod_kernel(c_ref, w_ref, b_ref, o_ref):
    c = c_ref[...]
    a = c * jax.nn.sigmoid(c)
    o_ref[0] = jnp.dot(a, w_ref[0], preferred_element_type=F32,
                       precision=lax.Precision.HIGHEST) + b_ref[0]


def _modulation(cc, w_mod, b_mod):
    depth, d, d6 = w_mod.shape
    rows = cc.shape[0]
    tn = d6 // 4
    return pl.pallas_call(
        _mod_kernel,
        grid=(depth, d6 // tn),
        in_specs=[
            pl.BlockSpec((rows, d), lambda l, j: (0, 0)),
            pl.BlockSpec((1, d, tn), lambda l, j: (l, 0, j)),
            pl.BlockSpec((1, 1, tn), lambda l, j: (l, 0, j)),
        ],
        out_specs=pl.BlockSpec((1, rows, tn), lambda l, j: (l, 0, j)),
        out_shape=jax.ShapeDtypeStruct((depth, rows, d6), F32),
        compiler_params=pltpu.CompilerParams(
            dimension_semantics=("arbitrary", "arbitrary"), vmem_limit_bytes=VMEM_LIMIT_BYTES),
        name="modulation",
    )(cc, w_mod, b_mod.reshape(depth, 1, d6))


def _rope(t, cos, sin_a, sin_b):
    quarter = HEAD_DIM // 4
    return (t * cos + pltpu.roll(t, LANES - quarter, 1) * sin_a + pltpu.roll(t, quarter, 1) * sin_b)


def _inproj_kernel(*refs, rope, kv_only, attn_w, kv_w, conv_w):
    if rope:
        x_ref, mod_ref, g_ref, w_ref, cos_ref, sa_ref, sb_ref = refs[:7]
        outs = refs[7:]
    else:
        x_ref, mod_ref, g_ref, w_ref = refs[:4]
        outs = refs[4:]
    tm = x_ref.shape[1]
    groups = [slice(r, r + tm // INPROJ_ROW_GROUPS) for r in range(0, tm, tm // INPROJ_ROW_GROUPS)]
    gain = g_ref[...] * (1.0 + mod_ref[0, 1:2, :])
    projected = []
    for rs in groups:
        hl = _rms_norm(x_ref[0, rs, :], gain) + mod_ref[0, 0:1, :]
        projected.append(_dot(hl.astype(BF16), w_ref[...]))
    low = lax.broadcasted_iota(jnp.int32, (1, LANES), 1) < HEAD_DIM
    high = jnp.logical_not(low)
    for rs, u in zip(groups, projected):
        def maybe_rope(t):
            if rope:
                return _rope(t, cos_ref[rs, :], sa_ref[rs, :], sb_ref[rs, :])
            return t

        if kv_only:
            kk_ref, vv_ref = outs
            off = 0
        else:
            q_ref, kk_ref, vv_ref, h_ref, p_ref = outs
            for c in range(attn_w // LANES):
                t = maybe_rope(u[:, c * LANES:(c + 1) * LANES])
                q_ref[0, rs, c * LANES:(c + 1) * LANES] = (t * (HEAD_DIM ** -0.5 * LOG2E)).astype(BF16)
            off = attn_w
        for c in range(kv_w // LANES):
            k = maybe_rope(u[:, off + c * LANES: off + (c + 1) * LANES])
            v = u[:, off + kv_w + c * LANES: off + kv_w + (c + 1) * LANES]
            for t, t_ref in ((k, kk_ref), (v, vv_ref)):
                swapped = pltpu.roll(t, HEAD_DIM, 1)
                for j, (keep, src) in enumerate(((low, t), (high, swapped), (low, swapped), (high, t))):
                    cols = slice((4 * c + j) * LANES, (4 * c + j + 1) * LANES)
                    t_ref[0, rs, cols] = jnp.where(keep, src, 0.0).astype(BF16)
        off += 2 * kv_w
        if not kv_only:
            a = u[:, off:off + conv_w]
            g = u[:, off + conv_w:off + 2 * conv_w]
            h_ref[0, rs, :] = a * jax.nn.sigmoid(g)
            off += 2 * conv_w
            p_ref[0, rs, :] = u[:, off:].astype(BF16)


def _inproj(x, mod, g, w, tables, *, tm, kv_only, dims):
    b, n, d = x.shape
    attn_w, kv_w, conv_w, pool_w = dims
    rope = tables is not None
    grid = (n // tm, b)
    in_specs = [
        pl.BlockSpec((1, tm, d), lambda i, bb: (bb, i, 0)),
        pl.BlockSpec((1, 6, d), lambda i, bb: (bb, 0, 0)),
        pl.BlockSpec((1, d), lambda i, bb: (0, 0)),
        pl.BlockSpec(w.shape, lambda i, bb: (0, 0)),
    ]
    args = [x, mod, g, w]
    if rope:
        in_specs += [pl.BlockSpec((tm, LANES), lambda i, bb: (i, 0))] * 3
        args += list(tables)
    tok = lambda width: pl.BlockSpec((1, tm, width), lambda i, bb: (bb, i, 0))
    out_specs = [tok(4 * kv_w), tok(4 * kv_w)]
    out_shape = [jax.ShapeDtypeStruct((b, n, 4 * kv_w), BF16)] * 2
    if not kv_only:
        out_specs = [tok(attn_w)] + out_specs + [tok(conv_w), tok(pool_w)]
        out_shape = ([jax.ShapeDtypeStruct((b, n, attn_w), BF16)] + out_shape
                     + [jax.ShapeDtypeStruct((b, n, conv_w), F32), jax.ShapeDtypeStruct((b, n, pool_w), BF16)])
    kern = functools.partial(_inproj_kernel, rope=rope, kv_only=kv_only,
                             attn_w=attn_w, kv_w=kv_w, conv_w=conv_w)
    return pl.pallas_call(
        kern, grid=grid, in_specs=in_specs, out_specs=out_specs, out_shape=out_shape,
        compiler_params=pltpu.CompilerParams(
            dimension_semantics=("arbitrary", "arbitrary"), vmem_limit_bytes=VMEM_LIMIT_BYTES),
        name="inproj_kv" if kv_only else "inproj",
    )(*args)


def _ffn_tile(d_ff):
    return MXU_COLS if d_ff % MXU_COLS == 0 else LANES


def _ffn_chunks(d_ff):
    unit = _ffn_tile(d_ff)
    units = d_ff // unit
    per = FFN_CHUNK_TILES
    return [(u * unit, min(per, units - u) * unit) for u in range(0, units, per)]


def _order_token(v):
    return (v[0:2 * SUBLANES, 0:LANES] > 0).astype(F32) * 0.0


def _attention_scores(q2, keys, lmask):
    rows = [k for pair in keys for k in pair if k is not None]
    s = _dot_nt(q2, jnp.concatenate(rows, axis=0))
    scores, off = [], 0
    for kcm, klm in keys:
        s_c = s[:, off:off + kcm.shape[0]]
        off += kcm.shape[0]
        s_l = None
        if klm is not None:
            s_l = jnp.where(lmask, s[:, off:off + klm.shape[0]], NEG)
            off += klm.shape[0]
        scores.append((s_c, s_l))
    return scores


def _attention_output(scores, values, sinks):
    weights, rows, inv = [], [], []
    for (s_c, s_l), (vcm, vlm), sink in zip(scores, values, sinks):
        mx = jnp.maximum(jnp.max(s_c, axis=-1, keepdims=True), sink)
        if s_l is not None:
            mx = jnp.maximum(mx, jnp.max(s_l, axis=-1, keepdims=True))
        e_c = jnp.exp2(s_c - mx)
        den = jnp.exp2(sink - mx) + jnp.sum(e_c, axis=-1, keepdims=True)
        weights.append(e_c.astype(BF16))
        rows.append(vcm)
        if s_l is not None:
            e_l = jnp.exp2(s_l - mx)
            den = den + jnp.sum(e_l, axis=-1, keepdims=True)
            weights.append(e_l.astype(BF16))
            rows.append(vlm)
        inv.append(1.0 / den)
    out = _dot(jnp.concatenate(weights, axis=1), jnp.concatenate(rows, axis=0))
    low = lax.broadcasted_iota(jnp.int32, (1, LANES), 1) < HEAD_DIM
    return out * jnp.where(low, inv[0], inv[1])


def _mixer_kernel(*refs, n, tm, has_local, final_norm, attn_w, conv_w, pool_w, d_ff):
    it = iter(refs)
    x_ref, q_ref = next(it), next(it)
    if has_local:
        kk_ref, vv_ref = next(it), next(it)
    kc_ref, vc_ref = next(it), next(it)
    h_ref, p_ref, mod_ref, sink_ref = (next(it) for _ in range(4))
    dw_ref, cvec_ref, band_ref, poolw_ref, wout_ref, g2_ref, wfi_ref, wfo_ref, gf_ref = (next(it) for _ in range(9))
    out_ref = next(it)
    mix_ref, hp_ref, hs_ref, y_ref, y2_ref = (next(it) for _ in range(5))

    s = pl.program_id(0)
    tiles = pl.num_programs(0) - 1
    tiles_per_seq = n // tm
    i = lax.rem(jnp.minimum(s, tiles - 1), tiles_per_seq)
    last_i = tiles_per_seq - 1
    t0 = pl.multiple_of(i * tm, tm)

    @pl.when(s == 0)
    def _():
        mix_ref[...] = jnp.zeros_like(mix_ref)

    ffn_state = {}

    def out_project():
        ffn_state["o"] = _dot(mix_ref[...], wout_ref[:, 0:x_ref.shape[2]])

    def ffn_norm():
        x1 = x_ref[0] + mod_ref[0, 2:3, :] * ffn_state["o"]
        y2 = _rms_norm(x1, g2_ref[...] * (1.0 + mod_ref[0, 4:5, :])) + mod_ref[0, 3:4, :]
        y2_ref[...] = y2.astype(BF16)
        ffn_state["x1"] = x1

    ffn_parts = []
    deferred = []
    tokens = []

    def ffn_stages(a, w):
        state = {}

        def hidden():
            if tokens:
                corner = (slice(0, 2 * SUBLANES), slice(0, LANES))
                y2_ref[corner] = y2_ref[corner] + sum(tokens).astype(BF16)
                tokens.clear()
            lhs = y2_ref[...]
            state["gate"] = _dot(lhs, wfi_ref[:, a:a + w])
            state["up"] = _dot(lhs, wfi_ref[:, d_ff + a:d_ff + a + w])

        def project():
            gate = state["gate"]
            act = (gate * jax.nn.sigmoid(gate) * state["up"]).astype(BF16)
            part = _dot(act, wfo_ref[a:a + w, 0:x_ref.shape[2]])
            ffn_parts[:] = [part if not ffn_parts else ffn_parts[0] + part]
        return hidden, project

    ffn_pieces = [ffn_stages(a, w) for a, w in _ffn_chunks(d_ff)]


    zero = jnp.zeros((), BF16)
    n_groups = kc_ref.shape[2] // (2 * LANES)
    heads_per_group = attn_w // HEAD_DIM // n_groups

    def attention_stages(blk, g, pr):
        state = {}
        rows = slice(blk * Q_BLOCK, (blk + 1) * Q_BLOCK)
        col = (g * heads_per_group // 2 + pr) * LANES
        head = g * heads_per_group + 2 * pr

        half_cols = [slice((2 * g + hh) * LANES, (2 * g + hh + 1) * LANES) for hh in range(2)]
        start = t0 + blk * Q_BLOCK
        s0 = pl.multiple_of(jnp.clip(start - WINDOW, 0, n - SPAN), Q_BLOCK) if has_local else None

        def scores():
            if has_local:
                r = lax.broadcasted_iota(jnp.int32, (Q_BLOCK, SPAN), 0)
                c = lax.broadcasted_iota(jnp.int32, (Q_BLOCK, SPAN), 1)
                dist = (start - s0) + r - c
                lmask = (dist >= -WINDOW) & (dist <= WINDOW)
            else:
                lmask = None
            keys = [(kc_ref[0, :, cs], kk_ref[0, pl.ds(s0, SPAN), cs] if has_local else None) for cs in half_cols]
            state["scores"] = _attention_scores(q_ref[0, rows, col:col + LANES], keys, lmask)

        def output():
            values = [(vc_ref[0, :, cs], vv_ref[0, pl.ds(s0, SPAN), cs] if has_local else None) for cs in half_cols]
            out = _attention_output(state["scores"], values, (sink_ref[head] * LOG2E, sink_ref[head + 1] * LOG2E))
            deferred.append((mix_ref, (rows, slice(col, col + LANES)), out.astype(BF16)))
        return scores, output

    attention_pieces = [attention_stages(blk, g, pr) for blk in range(tm // Q_BLOCK)
                        for g in range(n_groups) for pr in range(heads_per_group // 2)]

    dw_b, ln_g, ln_b, pool_scale = (cvec_ref[k:k + 1, :] for k in range(4))

    def conv_prepare():
        hp_ref[CONV_HALO:CONV_HALO + tm, :] = h_ref[0, pl.ds(t0, tm), :]
        prev = h_ref[0, pl.ds(pl.multiple_of(jnp.maximum(t0 - CONV_HALO, 0), CONV_HALO), CONV_HALO), :]
        hp_ref[0:CONV_HALO, :] = jnp.where(i > 0, prev, 0.0)
        nxt = h_ref[0, pl.ds(pl.multiple_of(jnp.minimum(t0 + tm, n - CONV_HALO), CONV_HALO), CONV_HALO), :]
        hp_ref[CONV_HALO + tm:, :] = jnp.where(i < last_i, nxt, 0.0)
        shifted_rows = tm + 2 * CONV_HALO - SUBLANES
        for sh in range(1, SUBLANES):
            hs_ref[sh - 1] = hp_ref[sh:sh + shifted_rows, :]

    def conv_piece(r0):
        def run():
            acc = jnp.zeros((ROW_CHUNK, conv_w), F32)
            for k in range(CONV_KERNEL):
                base = r0 + CONV_HALO - CONV_PAD + k
                sh = base % SUBLANES
                if sh == 0:
                    tap = hp_ref[base:base + ROW_CHUNK, :]
                else:
                    tap = hs_ref[sh - 1, base - sh:base - sh + ROW_CHUNK, :]
                acc = acc + tap * dw_ref[k:k + 1, :]
            hc = acc + dw_b
            mu = jnp.mean(hc, axis=-1, keepdims=True)
            cen = hc - mu
            var = jnp.mean(cen * cen, axis=-1, keepdims=True)
            hn = cen * lax.rsqrt(var + EPS) * ln_g + ln_b
            res = hn * jax.nn.sigmoid(hn)
            deferred.append((mix_ref, (slice(r0, r0 + ROW_CHUNK), slice(attn_w, attn_w + conv_w)), res.astype(BF16)))
            tokens.append(_order_token(res))
        return run

    group_w = pool_w // len(POOL_WINDOWS)

    def pool_block(blk):
        def run():
            start = t0 + blk * Q_BLOCK
            s0 = pl.multiple_of(jnp.clip(start - POOL_HALO, 0, n - POOL_SPAN), POOL_HALO)
            which = (start - s0) // POOL_HALO
            span = p_ref[0, pl.ds(s0, POOL_SPAN), :]
            lane = lax.broadcasted_iota(jnp.int32, (1, pool_w), 1)
            y = None
            for gi in range(len(POOL_WINDOWS)):
                in_group = (lane >= gi * group_w) & (lane < (gi + 1) * group_w)
                part = _dot(band_ref[which, gi], jnp.where(in_group, span, zero))
                y = part if y is None else y + part
            y_ref[blk * Q_BLOCK:(blk + 1) * Q_BLOCK, :] = y.astype(BF16)
        return run

    def pool_project():
        pooled = _dot(y_ref[...], poolw_ref[...]) * pool_scale
        deferred.append((mix_ref, (slice(None), slice(attn_w + conv_w, None)), pooled.astype(BF16)))

    n_chunks = len(ffn_pieces)
    per_group = -(-len(attention_pieces) // n_chunks)
    groups = [attention_pieces[j * per_group:(j + 1) * per_group] for j in range(n_chunks)]
    score_stage = lambda j: [scores for scores, _ in groups[j]]
    output_stage = lambda j: [output for _, output in groups[j]]
    hidden_stage = lambda j: [ffn_pieces[j][0]]
    project_stage = lambda j: [ffn_pieces[j][1]]
    elementwise = ([conv_prepare] + [conv_piece(r0) for r0 in range(0, tm, ROW_CHUNK)]
                   + [pool_block(blk) for blk in range(tm // Q_BLOCK)])
    per_elem = -(-len(elementwise) // max(n_chunks - 1, 1))
    order = [out_project, ffn_norm] + score_stage(0)
    for j in range(n_chunks):
        order += hidden_stage(j) + elementwise[j * per_elem:(j + 1) * per_elem] + output_stage(j)
        if j + 1 < n_chunks:
            order += score_stage(j + 1)
        order += project_stage(j)
    order += [pool_project]
    for stage in order:
        stage()

    x2 = ffn_state["x1"] + mod_ref[0, 5:6, :] * ffn_parts[0]
    if final_norm:
        x2 = _rms_norm(x2, gf_ref[...])
    out_ref[0] = x2
    for ref, index, value in deferred:
        ref[index] = value


def _mixer(x, q, kk, vv, kc, vc, h, p, mod, sink, dw, cvec, band, poolw, wout, g2, wfi, wfo, gf,
           *, tm, final_norm):
    b, n, d = x.shape
    has_local = kk is not None
    attn_w, kv_w, conv_w, pool_w = q.shape[2], kc.shape[2], h.shape[2], p.shape[2]
    n_ctx = kc.shape[1]
    d_ff = wfo.shape[0]
    tiles_per_seq = n // tm
    tiles = b * tiles_per_seq
    mix_tile = lambda s: jnp.minimum(s, tiles - 1)
    ffn_tile = lambda s: jnp.maximum(s - 1, 0)
    const = lambda shape: pl.BlockSpec(shape, lambda s: (0,) * len(shape), pipeline_mode=pl.Buffered(1))
    tok = lambda width, tile: pl.BlockSpec(
        (1, tm, width), lambda s: (tile(s) // tiles_per_seq, tile(s) % tiles_per_seq, 0))
    seq = lambda length, width, tile: pl.BlockSpec((1, length, width), lambda s: (tile(s) // tiles_per_seq, 0, 0))
    in_specs = [tok(d, ffn_tile), tok(attn_w, mix_tile)]
    args = [x, q]
    if has_local:
        in_specs += [seq(n, kv_w, mix_tile), seq(n, kv_w, mix_tile)]
        args += [kk, vv]
    in_specs += [seq(n_ctx, kv_w, mix_tile), seq(n_ctx, kv_w, mix_tile), seq(n, conv_w, mix_tile),
                 seq(n, pool_w, mix_tile), seq(6, d, ffn_tile),
                 pl.BlockSpec(memory_space=pltpu.SMEM),
                 const(dw.shape), const(cvec.shape), const(band.shape), const(poolw.shape), const(wout.shape),
                 const(g2.shape), const(wfi.shape), const(wfo.shape), const(gf.shape)]
    args += [kc, vc, h, p, mod, sink, dw, cvec, band, poolw, wout, g2, wfi, wfo, gf]
    kern = functools.partial(_mixer_kernel, n=n, tm=tm, has_local=has_local, final_norm=final_norm,
                             attn_w=attn_w, conv_w=conv_w, pool_w=pool_w, d_ff=d_ff)
    return pl.pallas_call(
        kern, grid=(tiles + 1,), in_specs=in_specs, out_specs=tok(d, ffn_tile),
        out_shape=jax.ShapeDtypeStruct((b, n, d), F32),
        scratch_shapes=[
            pltpu.VMEM((tm, attn_w + conv_w + pool_w), BF16),
            pltpu.VMEM((tm + 2 * CONV_HALO, conv_w), F32),
            pltpu.VMEM((SUBLANES - 1, tm + 2 * CONV_HALO - SUBLANES, conv_w), F32),
            pltpu.VMEM((tm, pool_w), BF16),
            pltpu.VMEM((tm, d), BF16),
        ],
        compiler_params=pltpu.CompilerParams(
            dimension_semantics=("arbitrary",), vmem_limit_bytes=VMEM_LIMIT_BYTES),
        name="mixer_local" if has_local else "mixer_ctx",
    )(*args)


def _rope_tables(n):
    rows = n // GRID_W
    row = jnp.repeat(jnp.arange(rows), GRID_W).astype(F32)
    col = jnp.tile(jnp.arange(GRID_W), rows).astype(F32)
    half = HEAD_DIM // 2
    inv = ROPE_BASE ** (-jnp.arange(0, half, 2, dtype=F32) / half)
    ar = row[:, None] * inv
    ac = col[:, None] * inv
    ang = jnp.concatenate([ar, ar, ac, ac], axis=-1)
    cos, sin = jnp.cos(ang), jnp.sin(ang)
    first = (jnp.arange(HEAD_DIM) // (HEAD_DIM // 4)) % 2 == 0
    sin_a = jnp.where(first, -sin, 0.0)
    sin_b = jnp.where(first, 0.0, sin)
    rep = LANES // HEAD_DIM
    return tuple(jnp.tile(t, (1, rep)) for t in (cos, sin_a, sin_b))


def _pool_bands(n):
    assert n >= POOL_SPAN and n % Q_BLOCK == 0
    r = jnp.arange(Q_BLOCK)[:, None]
    pos = jnp.arange(POOL_SPAN)[None, :]
    variants = []
    for k in range(3):
        t = k * POOL_HALO + r
        first = 0 if k == 0 else -POOL_SPAN
        last = POOL_SPAN - 1 if k == 2 else 2 * POOL_SPAN
        per_window = []
        for win in POOL_WINDOWS:
            lo = jnp.maximum(t - win // 2, first)
            hi = jnp.minimum(t + win - 1 - win // 2, last)
            inside = (pos >= lo) & (pos <= hi)
            mean = jnp.where(inside, 1.0 / (hi - lo + 1).astype(F32), 0.0)
            per_window.append(mean - (pos == t).astype(F32))
        variants.append(jnp.stack(per_window))
    return jnp.stack(variants).astype(BF16)


def _pad_lanes(w):
    return jnp.pad(w, ((0, 0), (0, LANES)))


def kernel(x, c, ctx, c_ctx, w_mod, b_mod, norm1_g, norm2_g, w_in, conv_dw, conv_dw_b, conv_ln_g, conv_ln_b,
           attn_sink, pool_w, pool_scale, w_out, w_ffn_in, w_ffn_out, final_g):
    b, n, d = x.shape
    n_ctx = ctx.shape[1]
    depth = w_mod.shape[0]
    heads = attn_sink.shape[1]
    attn_w = heads * HEAD_DIM
    conv_w = conv_dw.shape[2]
    pool_wd = pool_scale.shape[1]
    kv_w = w_in.shape[2] - attn_w - 2 * conv_w - pool_wd
    kv_w //= 2
    assert kv_w % LANES == 0
    dims = (attn_w, kv_w, conv_w, pool_wd)

    rows = -(-(b + 1) // SUBLANES) * SUBLANES
    cc = jnp.zeros((rows, d), F32).at[:b].set(c).at[b].set(c_ctx)
    mod_all = _modulation(cc, w_mod, b_mod)

    tables = _rope_tables(n)
    bands, bands_ctx = _pool_bands(n), _pool_bands(n_ctx)
    tm_in = min(512, n)
    tm_mix = min(256, n)
    cx = ctx
    for l in range(depth):
        last = l == depth - 1
        mod = mod_all[l, :b].reshape(b, 6, d)
        mod_c = jnp.broadcast_to(mod_all[l, b].reshape(1, 6, d), (b, 6, d))
        w_all = w_in[l].astype(BF16)
        w_kv = w_all[:, attn_w:attn_w + 2 * kv_w]
        g1 = norm1_g[l].reshape(1, d)
        g2 = norm2_g[l].reshape(1, d)
        cvec = jnp.zeros((SUBLANES, conv_w), F32).at[0].set(conv_dw_b[l]).at[1].set(conv_ln_g[l]) \
            .at[2].set(conv_ln_b[l]).at[3].set(pool_scale[l])
        poolw = jax.scipy.linalg.block_diag(*[pool_w[l, gi] for gi in range(pool_w.shape[1])]).astype(BF16)
        head = (attn_sink[l], conv_dw[l], cvec)
        shared = (poolw, _pad_lanes(w_out[l].astype(BF16)), g2,
                  w_ffn_in[l].astype(BF16), _pad_lanes(w_ffn_out[l].astype(BF16)), final_g.reshape(1, d))

        q, kk, vv, h, p = _inproj(x, mod, g1, w_all, tables, tm=tm_in, kv_only=False, dims=dims)
        if last:
            kc, vc = _inproj(cx, mod_c, g1, w_kv, None, tm=n_ctx, kv_only=True, dims=dims)
        else:
            qc, kc, vc, hc, pc = _inproj(cx, mod_c, g1, w_all, None, tm=n_ctx, kv_only=False, dims=dims)
        x = _mixer(x, q, kk, vv, kc, vc, h, p, mod, *head, bands, *shared, tm=tm_mix, final_norm=last)
        if not last:
            cx = _mixer(cx, qc, None, None, kc, vc, hc, pc, mod_c, *head, bands_ctx, *shared,
                        tm=n_ctx, final_norm=False)
    return x
```

```python
import functools

import jax
import jax.numpy as jnp
from jax import lax
from jax.experimental import pallas as pl
from jax.experimental.pallas import tpu as pltpu

F32 = jnp.float32
BF16 = jnp.bfloat16

LANES = 128
SUBLANES = 8
MXU_COLS = 256
FFN_CHUNK_TILES = 3
VMEM_LIMIT_BYTES = 62 * 1024 * 1024

GRID_W = 64
HEAD_DIM = 64
ROPE_BASE = 10000.0
WINDOW = 128
Q_BLOCK = 128
SPAN = Q_BLOCK + 2 * WINDOW
CONV_KERNEL = 31
CONV_PAD = CONV_KERNEL // 2
CONV_HALO = 16
POOL_WINDOWS = (2, 4, 8, 16)
POOL_HALO = 16
POOL_SPAN = Q_BLOCK + 2 * POOL_HALO
ROW_CHUNK = 64
INPROJ_ROW_GROUPS = 1
EPS = 1e-6
NEG = -1e30
LOG2E = 1.4426950408889634


def _rms_norm(x, g):
    return x * lax.rsqrt(jnp.mean(x * x, axis=-1, keepdims=True) + EPS) * g


_dot = functools.partial(jnp.dot, preferred_element_type=F32)
_dot_nt = functools.partial(lax.dot_general, dimension_numbers=(((1,), (1,)), ((), ())), preferred_element_type=F32)


def _mod_kernel(c_ref, w_ref, b_ref, o_ref):
    c = c_ref[...]
    a = c * jax.nn.sigmoid(c)
    o_ref[0] = jnp.dot(a, w_ref[0], preferred_element_type=F32,
                       precision=lax.Precision.HIGHEST) + b_ref[0]


def _modulation(cc, w_mod, b_mod):
    depth, d, d6 = w_mod.shape
    rows = cc.shape[0]
    tn = d6 // 4
    return pl.pallas_call(
        _mod_kernel,
        grid=(depth, d6 // tn),
        in_specs=[
            pl.BlockSpec((rows, d), lambda l, j: (0, 0)),
            pl.BlockSpec((1, d, tn), lambda l, j: (l, 0, j)),
            pl.BlockSpec((1, 1, tn), lambda l, j: (l, 0, j)),
        ],
        out_specs=pl.BlockSpec((1, rows, tn), lambda l, j: (l, 0, j)),
        out_shape=jax.ShapeDtypeStruct((depth, rows, d6), F32),
        compiler_params=pltpu.CompilerParams(
            dimension_semantics=("arbitrary", "arbitrary"), vmem_limit_bytes=VMEM_LIMIT_BYTES),
        name="modulation",
    )(cc, w_mod, b_mod.reshape(depth, 1, d6))


def _rope(t, cos, sin_a, sin_b):
    quarter = HEAD_DIM // 4
    return (t * cos + pltpu.roll(t, LANES - quarter, 1) * sin_a + pltpu.roll(t, quarter, 1) * sin_b)


def _inproj_kernel(*refs, rope, kv_only, attn_w, kv_w, conv_w):
    if rope:
        x_ref, mod_ref, g_ref, w_ref, cos_ref, sa_ref, sb_ref = refs[:7]
        outs = refs[7:]
    else:
        x_ref, mod_ref, g_ref, w_ref = refs[:4]
        outs = refs[4:]
    tm = x_ref.shape[1]
    groups = [slice(r, r + tm // INPROJ_ROW_GROUPS) for r in range(0, tm, tm // INPROJ_ROW_GROUPS)]
    gain = g_ref[...] * (1.0 + mod_ref[0, 1:2, :])
    projected = []
    for rs in groups:
        hl = _rms_norm(x_ref[0, rs, :], gain) + mod_ref[0, 0:1, :]
        projected.append(_dot(hl.astype(BF16), w_ref[...]))
    low = lax.broadcasted_iota(jnp.int32, (1, LANES), 1) < HEAD_DIM
    high = jnp.logical_not(low)
    for rs, u in zip(groups, projected):
        def maybe_rope(t):
            if rope:
                return _rope(t, cos_ref[rs, :], sa_ref[rs, :], sb_ref[rs, :])
            return t

        if kv_only:
            kk_ref, vv_ref = outs
            off = 0
        else:
            q_ref, kk_ref, vv_ref, h_ref, p_ref = outs
            for c in range(attn_w // LANES):
                t = maybe_rope(u[:, c * LANES:(c + 1) * LANES])
                q_ref[0, rs, c * LANES:(c + 1) * LANES] = (t * (HEAD_DIM ** -0.5 * LOG2E)).astype(BF16)
            off = attn_w
        for c in range(kv_w // LANES):
            k = maybe_rope(u[:, off + c * LANES: off + (c + 1) * LANES])
            v = u[:, off + kv_w + c * LANES: off + kv_w + (c + 1) * LANES]
            for t, t_ref in ((k, kk_ref), (v, vv_ref)):
                swapped = pltpu.roll(t, HEAD_DIM, 1)
                for j, (keep, src) in enumerate(((low, t), (high, swapped), (low, swapped), (high, t))):
                    cols = slice((4 * c + j) * LANES, (4 * c + j + 1) * LANES)
                    t_ref[0, rs, cols] = jnp.where(keep, src, 0.0).astype(BF16)
        off += 2 * kv_w
        if not kv_only:
            a = u[:, off:off + conv_w]
            g = u[:, off + conv_w:off + 2 * conv_w]
            h_ref[0, rs, :] = (a * jax.nn.sigmoid(g)).astype(BF16)
            off += 2 * conv_w
            p_ref[0, rs, :] = u[:, off:].astype(BF16)


def _inproj(x, mod, g, w, tables, *, tm, kv_only, dims):
    b, n, d = x.shape
    attn_w, kv_w, conv_w, pool_w = dims
    rope = tables is not None
    grid = (n // tm, b)
    in_specs = [
        pl.BlockSpec((1, tm, d), lambda i, bb: (bb, i, 0)),
        pl.BlockSpec((1, 6, d), lambda i, bb: (bb, 0, 0)),
        pl.BlockSpec((1, d), lambda i, bb: (0, 0)),
        pl.BlockSpec(w.shape, lambda i, bb: (0, 0)),
    ]
    args = [x, mod, g, w]
    if rope:
        in_specs += [pl.BlockSpec((tm, LANES), lambda i, bb: (i, 0))] * 3
        args += list(tables)
    tok = lambda width: pl.BlockSpec((1, tm, width), lambda i, bb: (bb, i, 0))
    out_specs = [tok(4 * kv_w), tok(4 * kv_w)]
    out_shape = [jax.ShapeDtypeStruct((b, n, 4 * kv_w), BF16)] * 2
    if not kv_only:
        out_specs = [tok(attn_w)] + out_specs + [tok(conv_w), tok(pool_w)]
        out_shape = ([jax.ShapeDtypeStruct((b, n, attn_w), BF16)] + out_shape
                     + [jax.ShapeDtypeStruct((b, n, conv_w), BF16), jax.ShapeDtypeStruct((b, n, pool_w), BF16)])
    kern = functools.partial(_inproj_kernel, rope=rope, kv_only=kv_only,
                             attn_w=attn_w, kv_w=kv_w, conv_w=conv_w)
    return pl.pallas_call(
        kern, grid=grid, in_specs=in_specs, out_specs=out_specs, out_shape=out_shape,
        compiler_params=pltpu.CompilerParams(
            dimension_semantics=("arbitrary", "arbitrary"), vmem_limit_bytes=VMEM_LIMIT_BYTES),
        name="inproj_kv" if kv_only else "inproj",
    )(*args)


def _ffn_tile(d_ff):
    return MXU_COLS if d_ff % MXU_COLS == 0 else LANES


def _ffn_chunks(d_ff):
    unit = _ffn_tile(d_ff)
    units = d_ff // unit
    per = FFN_CHUNK_TILES
    return [(u * unit, min(per, units - u) * unit) for u in range(0, units, per)]


def _order_token(v):
    return (v[0:2 * SUBLANES, 0:LANES] > 0).astype(F32) * 0.0


def _attention_scores(q2, keys, lmask):
    rows = [k for pair in keys for k in pair if k is not None]
    s = _dot_nt(q2, jnp.concatenate(rows, axis=0))
    scores, off = [], 0
    for kcm, klm in keys:
        s_c = s[:, off:off + kcm.shape[0]]
        off += kcm.shape[0]
        s_l = None
        if klm is not None:
            s_l = jnp.where(lmask, s[:, off:off + klm.shape[0]], NEG)
            off += klm.shape[0]
        scores.append((s_c, s_l))
    return scores


def _attention_output(scores, values, sinks):
    weights, rows, inv = [], [], []
    for (s_c, s_l), (vcm, vlm), sink in zip(scores, values, sinks):
        mx = jnp.maximum(jnp.max(s_c, axis=-1, keepdims=True), sink)
        if s_l is not None:
            mx = jnp.maximum(mx, jnp.max(s_l, axis=-1, keepdims=True))
        e_c = jnp.exp2(s_c - mx)
        den = jnp.exp2(sink - mx) + jnp.sum(e_c, axis=-1, keepdims=True)
        weights.append(e_c.astype(BF16))
        rows.append(vcm)
        if s_l is not None:
            e_l = jnp.exp2(s_l - mx)
            den = den + jnp.sum(e_l, axis=-1, keepdims=True)
            weights.append(e_l.astype(BF16))
            rows.append(vlm)
        inv.append(1.0 / den)
    out = _dot(jnp.concatenate(weights, axis=1), jnp.concatenate(rows, axis=0))
    low = lax.broadcasted_iota(jnp.int32, (1, LANES), 1) < HEAD_DIM
    return out * jnp.where(low, inv[0], inv[1])


def _mixer_kernel(*refs, n, tm, has_local, final_norm, attn_w, conv_w, pool_w, d_ff):
    it = iter(refs)
    x_ref, q_ref = next(it), next(it)
    if has_local:
        kk_ref, vv_ref = next(it), next(it)
    kc_ref, vc_ref = next(it), next(it)
    h_ref, p_ref, mod_ref, sink_ref = (next(it) for _ in range(4))
    dw_ref, cvec_ref, band_ref, poolw_ref, wout_ref, g2_ref, wfi_ref, wfo_ref, gf_ref = (next(it) for _ in range(9))
    out_ref = next(it)
    mix_ref, hp_ref, hs_ref, y_ref, y2_ref = (next(it) for _ in range(5))

    s = pl.program_id(0)
    tiles = pl.num_programs(0) - 1
    tiles_per_seq = n // tm
    i = lax.rem(jnp.minimum(s, tiles - 1), tiles_per_seq)
    last_i = tiles_per_seq - 1
    t0 = pl.multiple_of(i * tm, tm)

    @pl.when(s == 0)
    def _():
        mix_ref[...] = jnp.zeros_like(mix_ref)

    ffn_state = {}

    def out_project():
        ffn_state["o"] = _dot(mix_ref[...], wout_ref[:, 0:x_ref.shape[2]])

    def ffn_norm():
        x1 = x_ref[0] + mod_ref[0, 2:3, :] * ffn_state["o"]
        y2 = _rms_norm(x1, g2_ref[...] * (1.0 + mod_ref[0, 4:5, :])) + mod_ref[0, 3:4, :]
        y2_ref[...] = y2.astype(BF16)
        ffn_state["x1"] = x1

    ffn_parts = []
    deferred = []
    tokens = []

    def ffn_stages(a, w):
        state = {}

        def hidden():
            if tokens:
                corner = (slice(0, 2 * SUBLANES), slice(0, LANES))
                y2_ref[corner] = y2_ref[corner] + sum(tokens).astype(BF16)
                tokens.clear()
            lhs = y2_ref[...]
            state["gate"] = _dot(lhs, wfi_ref[:, a:a + w])
            state["up"] = _dot(lhs, wfi_ref[:, d_ff + a:d_ff + a + w])

        def project():
            gate = state["gate"]
            act = (gate * jax.nn.sigmoid(gate) * state["up"]).astype(BF16)
            part = _dot(act, wfo_ref[a:a + w, 0:x_ref.shape[2]])
            ffn_parts[:] = [part if not ffn_parts else ffn_parts[0] + part]
        return hidden, project

    ffn_pieces = [ffn_stages(a, w) for a, w in _ffn_chunks(d_ff)]


    zero = jnp.zeros((), BF16)
    n_groups = kc_ref.shape[2] // (2 * LANES)
    heads_per_group = attn_w // HEAD_DIM // n_groups

    def attention_stages(blk, g, pr):
        state = {}
        rows = slice(blk * Q_BLOCK, (blk + 1) * Q_BLOCK)
        col = (g * heads_per_group // 2 + pr) * LANES
        head = g * heads_per_group + 2 * pr

        half_cols = [slice((2 * g + hh) * LANES, (2 * g + hh + 1) * LANES) for hh in range(2)]
        start = t0 + blk * Q_BLOCK
        s0 = pl.multiple_of(jnp.clip(start - WINDOW, 0, n - SPAN), Q_BLOCK) if has_local else None

        def scores():
            if has_local:
                r = lax.broadcasted_iota(jnp.int32, (Q_BLOCK, SPAN), 0)
                c = lax.broadcasted_iota(jnp.int32, (Q_BLOCK, SPAN), 1)
                dist = (start - s0) + r - c
                lmask = (dist >= -WINDOW) & (dist <= WINDOW)
            else:
                lmask = None
            keys = [(kc_ref[0, :, cs], kk_ref[0, pl.ds(s0, SPAN), cs] if has_local else None) for cs in half_cols]
            state["scores"] = _attention_scores(q_ref[0, rows, col:col + LANES], keys, lmask)

        def output():
            values = [(vc_ref[0, :, cs], vv_ref[0, pl.ds(s0, SPAN), cs] if has_local else None) for cs in half_cols]
            out = _attention_output(state["scores"], values, (sink_ref[head] * LOG2E, sink_ref[head + 1] * LOG2E))
            deferred.append((mix_ref, (rows, slice(col, col + LANES)), out.astype(BF16)))
        return scores, output

    attention_pieces = [attention_stages(blk, g, pr) for blk in range(tm // Q_BLOCK)
                        for g in range(n_groups) for pr in range(heads_per_group // 2)]

    dw_b, ln_g, ln_b, pool_scale = (cvec_ref[k:k + 1, :] for k in range(4))

    def conv_prepare():
        hp_ref[CONV_HALO:CONV_HALO + tm, :] = h_ref[0, pl.ds(t0, tm), :].astype(F32)
        prev = h_ref[0, pl.ds(pl.multiple_of(jnp.maximum(t0 - CONV_HALO, 0), CONV_HALO), CONV_HALO), :]
        hp_ref[0:CONV_HALO, :] = jnp.where(i > 0, prev.astype(F32), 0.0)
        nxt = h_ref[0, pl.ds(pl.multiple_of(jnp.minimum(t0 + tm, n - CONV_HALO), CONV_HALO), CONV_HALO), :]
        hp_ref[CONV_HALO + tm:, :] = jnp.where(i < last_i, nxt.astype(F32), 0.0)

    def conv_piece(r0):
        def run():
            for sh in range(1, SUBLANES):
                hs_ref[sh - 1] = hp_ref[r0 + sh:r0 + sh + hs_ref.shape[1], :]
            acc = jnp.zeros((ROW_CHUNK, conv_w), F32)
            for k in range(CONV_KERNEL):
                base = CONV_HALO - CONV_PAD + k
                sh = base % SUBLANES
                if sh == 0:
                    tap = hp_ref[r0 + base:r0 + base + ROW_CHUNK, :]
                else:
                    tap = hs_ref[sh - 1, base - sh:base - sh + ROW_CHUNK, :]
                acc = acc + tap * dw_ref[k:k + 1, :]
            hc = acc + dw_b
            mu = jnp.mean(hc, axis=-1, keepdims=True)
            cen = hc - mu
            var = jnp.mean(cen * cen, axis=-1, keepdims=True)
            hn = cen * lax.rsqrt(var + EPS) * ln_g + ln_b
            res = hn * jax.nn.sigmoid(hn)
            deferred.append((mix_ref, (slice(r0, r0 + ROW_CHUNK), slice(attn_w, attn_w + conv_w)), res.astype(BF16)))
            tokens.append(_order_token(res))
        return run

    group_w = pool_w // len(POOL_WINDOWS)

    def pool_block(blk):
        def run():
            start = t0 + blk * Q_BLOCK
            s0 = pl.multiple_of(jnp.clip(start - POOL_HALO, 0, n - POOL_SPAN), POOL_HALO)
            which = (start - s0) // POOL_HALO
            span = p_ref[0, pl.ds(s0, POOL_SPAN), :]
            lane = lax.broadcasted_iota(jnp.int32, (1, pool_w), 1)
            y = None
            for gi in range(len(POOL_WINDOWS)):
                in_group = (lane >= gi * group_w) & (lane < (gi + 1) * group_w)
                part = _dot(band_ref[which, gi], jnp.where(in_group, span, zero))
                y = part if y is None else y + part
            y_ref[blk * Q_BLOCK:(blk + 1) * Q_BLOCK, :] = y.astype(BF16)
        return run

    def pool_project():
        pooled = _dot(y_ref[...], poolw_ref[...]) * pool_scale
        deferred.append((mix_ref, (slice(None), slice(attn_w + conv_w, None)), pooled.astype(BF16)))

    n_chunks = len(ffn_pieces)
    per_group = -(-len(attention_pieces) // n_chunks)
    groups = [attention_pieces[j * per_group:(j + 1) * per_group] for j in range(n_chunks)]
    score_stage = lambda j: [scores for scores, _ in groups[j]]
    output_stage = lambda j: [output for _, output in groups[j]]
    hidden_stage = lambda j: [ffn_pieces[j][0]]
    project_stage = lambda j: [ffn_pieces[j][1]]
    elementwise = ([conv_prepare] + [conv_piece(r0) for r0 in range(0, tm, ROW_CHUNK)]
                   + [pool_block(blk) for blk in range(tm // Q_BLOCK)])
    per_elem = -(-len(elementwise) // max(n_chunks - 1, 1))
    order = [out_project, ffn_norm] + score_stage(0)
    for j in range(n_chunks):
        order += hidden_stage(j) + elementwise[j * per_elem:(j + 1) * per_elem] + output_stage(j)
        if j + 1 < n_chunks:
            order += score_stage(j + 1)
        order += project_stage(j)
    order += [pool_project]
    for stage in order:
        stage()

    x2 = ffn_state["x1"] + mod_ref[0, 5:6, :] * ffn_parts[0]
    if final_norm:
        x2 = _rms_norm(x2, gf_ref[...])
    out_ref[0] = x2
    for ref, index, value in deferred:
        ref[index] = value


def _mixer(x, q, kk, vv, kc, vc, h, p, mod, sink, dw, cvec, band, poolw, wout, g2, wfi, wfo, gf,
           *, tm, final_norm):
    b, n, d = x.shape
    has_local = kk is not None
    attn_w, kv_w, conv_w, pool_w = q.shape[2], kc.shape[2], h.shape[2], p.shape[2]
    n_ctx = kc.shape[1]
    d_ff = wfo.shape[0]
    tiles_per_seq = n // tm
    tiles = b * tiles_per_seq
    mix_tile = lambda s: jnp.minimum(s, tiles - 1)
    ffn_tile = lambda s: jnp.maximum(s - 1, 0)
    const = lambda shape: pl.BlockSpec(shape, lambda s: (0,) * len(shape), pipeline_mode=pl.Buffered(1))
    tok = lambda width, tile: pl.BlockSpec(
        (1, tm, width), lambda s: (tile(s) // tiles_per_seq, tile(s) % tiles_per_seq, 0))
    seq = lambda length, width, tile: pl.BlockSpec((1, length, width), lambda s: (tile(s) // tiles_per_seq, 0, 0))
    in_specs = [tok(d, ffn_tile), tok(attn_w, mix_tile)]
    args = [x, q]
    if has_local:
        in_specs += [seq(n, kv_w, mix_tile), seq(n, kv_w, mix_tile)]
        args += [kk, vv]
    in_specs += [seq(n_ctx, kv_w, mix_tile), seq(n_ctx, kv_w, mix_tile), seq(n, conv_w, mix_tile),
                 seq(n, pool_w, mix_tile), seq(6, d, ffn_tile),
                 pl.BlockSpec(memory_space=pltpu.SMEM),
                 const(dw.shape), const(cvec.shape), const(band.shape), const(poolw.shape), const(wout.shape),
                 const(g2.shape), const(wfi.shape), const(wfo.shape), const(gf.shape)]
    args += [kc, vc, h, p, mod, sink, dw, cvec, band, poolw, wout, g2, wfi, wfo, gf]
    kern = functools.partial(_mixer_kernel, n=n, tm=tm, has_local=has_local, final_norm=final_norm,
                             attn_w=attn_w, conv_w=conv_w, pool_w=pool_w, d_ff=d_ff)
    return pl.pallas_call(
        kern, grid=(tiles + 1,), in_specs=in_specs, out_specs=tok(d, ffn_tile),
        out_shape=jax.ShapeDtypeStruct((b, n, d), F32),
        scratch_shapes=[
            pltpu.VMEM((tm, attn_w + conv_w + pool_w), BF16),
            pltpu.VMEM((tm + 2 * CONV_HALO, conv_w), F32),
            pltpu.VMEM((SUBLANES - 1, ROW_CHUNK + 2 * CONV_HALO - SUBLANES, conv_w), F32),
            pltpu.VMEM((tm, pool_w), BF16),
            pltpu.VMEM((tm, d), BF16),
        ],
        compiler_params=pltpu.CompilerParams(
            dimension_semantics=("arbitrary",), vmem_limit_bytes=VMEM_LIMIT_BYTES),
        name="mixer_local" if has_local else "mixer_ctx",
    )(*args)


def _rope_tables(n):
    rows = n // GRID_W
    row = jnp.repeat(jnp.arange(rows), GRID_W).astype(F32)
    col = jnp.tile(jnp.arange(GRID_W), rows).astype(F32)
    half = HEAD_DIM // 2
    inv = ROPE_BASE ** (-jnp.arange(0, half, 2, dtype=F32) / half)
    ar = row[:, None] * inv
    ac = col[:, None] * inv
    ang = jnp.concatenate([ar, ar, ac, ac], axis=-1)
    cos, sin = jnp.cos(ang), jnp.sin(ang)
    first = (jnp.arange(HEAD_DIM) // (HEAD_DIM // 4)) % 2 == 0
    sin_a = jnp.where(first, -sin, 0.0)
    sin_b = jnp.where(first, 0.0, sin)
    rep = LANES // HEAD_DIM
    return tuple(jnp.tile(t, (1, rep)) for t in (cos, sin_a, sin_b))


def _pool_bands(n):
    assert n >= POOL_SPAN and n % Q_BLOCK == 0
    r = jnp.arange(Q_BLOCK)[:, None]
    pos = jnp.arange(POOL_SPAN)[None, :]
    variants = []
    for k in range(3):
        t = k * POOL_HALO + r
        first = 0 if k == 0 else -POOL_SPAN
        last = POOL_SPAN - 1 if k == 2 else 2 * POOL_SPAN
        per_window = []
        for win in POOL_WINDOWS:
            lo = jnp.maximum(t - win // 2, first)
            hi = jnp.minimum(t + win - 1 - win // 2, last)
            inside = (pos >= lo) & (pos <= hi)
            mean = jnp.where(inside, 1.0 / (hi - lo + 1).astype(F32), 0.0)
            per_window.append(mean - (pos == t).astype(F32))
        variants.append(jnp.stack(per_window))
    return jnp.stack(variants).astype(BF16)


def _pad_lanes(w):
    return jnp.pad(w, ((0, 0), (0, LANES)))


def kernel(x, c, ctx, c_ctx, w_mod, b_mod, norm1_g, norm2_g, w_in, conv_dw, conv_dw_b, conv_ln_g, conv_ln_b,
           attn_sink, pool_w, pool_scale, w_out, w_ffn_in, w_ffn_out, final_g):
    b, n, d = x.shape
    n_ctx = ctx.shape[1]
    depth = w_mod.shape[0]
    heads = attn_sink.shape[1]
    attn_w = heads * HEAD_DIM
    conv_w = conv_dw.shape[2]
    pool_wd = pool_scale.shape[1]
    kv_w = w_in.shape[2] - attn_w - 2 * conv_w - pool_wd
    kv_w //= 2
    assert kv_w % LANES == 0
    dims = (attn_w, kv_w, conv_w, pool_wd)

    rows = -(-(b + 1) // SUBLANES) * SUBLANES
    cc = jnp.zeros((rows, d), F32).at[:b].set(c).at[b].set(c_ctx)
    mod_all = _modulation(cc, w_mod, b_mod)

    tables = _rope_tables(n)
    bands, bands_ctx = _pool_bands(n), _pool_bands(n_ctx)
    tm_in = min(512, n)
    tm_mix = min(512, n)
    cx = ctx
    for l in range(depth):
        last = l == depth - 1
        mod = mod_all[l, :b].reshape(b, 6, d)
        mod_c = jnp.broadcast_to(mod_all[l, b].reshape(1, 6, d), (b, 6, d))
        w_all = w_in[l].astype(BF16)
        w_kv = w_all[:, attn_w:attn_w + 2 * kv_w]
        g1 = norm1_g[l].reshape(1, d)
        g2 = norm2_g[l].reshape(1, d)
        cvec = jnp.zeros((SUBLANES, conv_w), F32).at[0].set(conv_dw_b[l]).at[1].set(conv_ln_g[l]) \
            .at[2].set(conv_ln_b[l]).at[3].set(pool_scale[l])
        poolw = jax.scipy.linalg.block_diag(*[pool_w[l, gi] for gi in range(pool_w.shape[1])]).astype(BF16)
        head = (attn_sink[l], conv_dw[l], cvec)
        shared = (poolw, _pad_lanes(w_out[l].astype(BF16)), g2,
                  w_ffn_in[l].astype(BF16), _pad_lanes(w_ffn_out[l].astype(BF16)), final_g.reshape(1, d))

        q, kk, vv, h, p = _inproj(x, mod, g1, w_all, tables, tm=tm_in, kv_only=False, dims=dims)
        if last:
            kc, vc = _inproj(cx, mod_c, g1, w_kv, None, tm=n_ctx, kv_only=True, dims=dims)
        else:
            qc, kc, vc, hc, pc = _inproj(cx, mod_c, g1, w_all, None, tm=n_ctx, kv_only=False, dims=dims)
        x = _mixer(x, q, kk, vv, kc, vc, h, p, mod, *head, bands, *shared, tm=tm_mix, final_norm=last)
        if not last:
            cx = _mixer(cx, qc, None, None, kc, vc, hc, pc, mod_c, *head, bands_ctx, *shared,
                        tm=n_ctx, final_norm=False)
    return x
```

```python
import functools

import jax
import jax.numpy as jnp
from jax import lax
from jax.experimental import pallas as pl
from jax.experimental.pallas import tpu as pltpu

F32 = jnp.float32
BF16 = jnp.bfloat16

LANES = 128
SUBLANES = 8
MXU_COLS = 256
FFN_CHUNK_TILES = 3
VMEM_LIMIT_BYTES = 62 * 1024 * 1024

GRID_W = 64
HEAD_DIM = 64
ROPE_BASE = 10000.0
WINDOW = 128
Q_BLOCK = 128
SPAN = Q_BLOCK + 2 * WINDOW
CONV_KERNEL = 31
CONV_PAD = CONV_KERNEL // 2
CONV_HALO = 16
POOL_WINDOWS = (2, 4, 8, 16)
POOL_HALO = 16
POOL_SPAN = Q_BLOCK + 2 * POOL_HALO
ROW_CHUNK = 64
INPROJ_ROW_GROUPS = 1
EPS = 1e-6
NEG = -1e30
LOG2E = 1.4426950408889634


def _rms_norm(x, g):
    return x * lax.rsqrt(jnp.mean(x * x, axis=-1, keepdims=True) + EPS) * g


_dot = functools.partial(jnp.dot, preferred_element_type=F32)
_dot_nt = functools.partial(lax.dot_general, dimension_numbers=(((1,), (1,)), ((), ())), preferred_element_type=F32)


def _mod_kernel(c_ref, w_ref, b_ref, o_ref):
    c = c_ref[...]
    a = c * jax.nn.sigmoid(c)
    o_ref[0] = jnp.dot(a, w_ref[0], preferred_element_type=F32,
                       precision=lax.Precision.HIGHEST) + b_ref[0]


def _modulation(cc, w_mod, b_mod):
    depth, d, d6 = w_mod.shape
    rows = cc.shape[0]
    tn = d6 // 4
    return pl.pallas_call(
        _mod_kernel,
        grid=(depth, d6 // tn),
        in_specs=[
            pl.BlockSpec((rows, d), lambda l, j: (0, 0)),
            pl.BlockSpec((1, d, tn), lambda l, j: (l, 0, j)),
            pl.BlockSpec((1, 1, tn), lambda l, j: (l, 0, j)),
        ],
        out_specs=pl.BlockSpec((1, rows, tn), lambda l, j: (l, 0, j)),
        out_shape=jax.ShapeDtypeStruct((depth, rows, d6), F32),
        compiler_params=pltpu.CompilerParams(
            dimension_semantics=("arbitrary", "arbitrary"), vmem_limit_bytes=VMEM_LIMIT_BYTES),
        name="modulation",
    )(cc, w_mod, b_mod.reshape(depth, 1, d6))


def _rope(t, cos, sin_a, sin_b):
    quarter = HEAD_DIM // 4
    return (t * cos + pltpu.roll(t, LANES - quarter, 1) * sin_a + pltpu.roll(t, quarter, 1) * sin_b)


def _inproj_kernel(*refs, rope, kv_only, attn_w, kv_w, conv_w):
    if rope:
        x_ref, mod_ref, g_ref, w_ref, cos_ref, sa_ref, sb_ref = refs[:7]
        outs = refs[7:]
    else:
        x_ref, mod_ref, g_ref, w_ref = refs[:4]
        outs = refs[4:]
    tm = x_ref.shape[1]
    groups = [slice(r, r + tm // INPROJ_ROW_GROUPS) for r in range(0, tm, tm // INPROJ_ROW_GROUPS)]
    gain = g_ref[...] * (1.0 + mod_ref[0, 1:2, :])
    projected = []
    for rs in groups:
        hl = _rms_norm(x_ref[0, rs, :], gain) + mod_ref[0, 0:1, :]
        projected.append(_dot(hl.astype(BF16), w_ref[...]))
    low = lax.broadcasted_iota(jnp.int32, (1, LANES), 1) < HEAD_DIM
    high = jnp.logical_not(low)
    for rs, u in zip(groups, projected):
        def maybe_rope(t):
            if rope:
                return _rope(t, cos_ref[rs, :], sa_ref[rs, :], sb_ref[rs, :])
            return t

        if kv_only:
            kk_ref, vv_ref = outs
            off = 0
        else:
            q_ref, kk_ref, vv_ref, h_ref, p_ref = outs
            for c in range(attn_w // LANES):
                t = maybe_rope(u[:, c * LANES:(c + 1) * LANES])
                q_ref[0, rs, c * LANES:(c + 1) * LANES] = (t * (HEAD_DIM ** -0.5 * LOG2E)).astype(BF16)
            off = attn_w
        for c in range(kv_w // LANES):
            k = maybe_rope(u[:, off + c * LANES: off + (c + 1) * LANES])
            v = u[:, off + kv_w + c * LANES: off + kv_w + (c + 1) * LANES]
            for t, t_ref in ((k, kk_ref), (v, vv_ref)):
                swapped = pltpu.roll(t, HEAD_DIM, 1)
                for j, (keep, src) in enumerate(((low, t), (high, swapped), (low, swapped), (high, t))):
                    cols = slice((4 * c + j) * LANES, (4 * c + j + 1) * LANES)
                    t_ref[0, rs, cols] = jnp.where(keep, src, 0.0).astype(BF16)
        off += 2 * kv_w
        if not kv_only:
            a = u[:, off:off + conv_w]
            g = u[:, off + conv_w:off + 2 * conv_w]
            h_ref[0, rs, :] = (a * jax.nn.sigmoid(g)).astype(BF16)
            off += 2 * conv_w
            p_ref[0, rs, :] = u[:, off:].astype(BF16)


def _inproj(x, mod, g, w, tables, *, tm, kv_only, dims):
    b, n, d = x.shape
    attn_w, kv_w, conv_w, pool_w = dims
    rope = tables is not None
    grid = (n // tm, b)
    in_specs = [
        pl.BlockSpec((1, tm, d), lambda i, bb: (bb, i, 0)),
        pl.BlockSpec((1, 6, d), lambda i, bb: (bb, 0, 0)),
        pl.BlockSpec((1, d), lambda i, bb: (0, 0)),
        pl.BlockSpec(w.shape, lambda i, bb: (0, 0)),
    ]
    args = [x, mod, g, w]
    if rope:
        in_specs += [pl.BlockSpec((tm, LANES), lambda i, bb: (i, 0))] * 3
        args += list(tables)
    tok = lambda width: pl.BlockSpec((1, tm, width), lambda i, bb: (bb, i, 0))
    out_specs = [tok(4 * kv_w), tok(4 * kv_w)]
    out_shape = [jax.ShapeDtypeStruct((b, n, 4 * kv_w), BF16)] * 2
    if not kv_only:
        out_specs = [tok(attn_w)] + out_specs + [tok(conv_w), tok(pool_w)]
        out_shape = ([jax.ShapeDtypeStruct((b, n, attn_w), BF16)] + out_shape
                     + [jax.ShapeDtypeStruct((b, n, conv_w), BF16), jax.ShapeDtypeStruct((b, n, pool_w), BF16)])
    kern = functools.partial(_inproj_kernel, rope=rope, kv_only=kv_only,
                             attn_w=attn_w, kv_w=kv_w, conv_w=conv_w)
    return pl.pallas_call(
        kern, grid=grid, in_specs=in_specs, out_specs=out_specs, out_shape=out_shape,
        compiler_params=pltpu.CompilerParams(
            dimension_semantics=("arbitrary", "arbitrary"), vmem_limit_bytes=VMEM_LIMIT_BYTES),
        name="inproj_kv" if kv_only else "inproj",
    )(*args)


def _ffn_tile(d_ff):
    return MXU_COLS if d_ff % MXU_COLS == 0 else LANES


def _ffn_chunks(d_ff):
    unit = _ffn_tile(d_ff)
    units = d_ff // unit
    per = FFN_CHUNK_TILES
    return [(u * unit, min(per, units - u) * unit) for u in range(0, units, per)]


def _order_token(v):
    return (v[0:2 * SUBLANES, 0:LANES] > 0).astype(F32) * 0.0


def _attention_scores(q2, keys, lmask):
    rows = [k for pair in keys for k in pair if k is not None]
    s = _dot_nt(q2, jnp.concatenate(rows, axis=0))
    scores, off = [], 0
    for kcm, klm in keys:
        s_c = s[:, off:off + kcm.shape[0]]
        off += kcm.shape[0]
        s_l = None
        if klm is not None:
            s_l = jnp.where(lmask, s[:, off:off + klm.shape[0]], NEG)
            off += klm.shape[0]
        scores.append((s_c, s_l))
    return scores


def _attention_output(scores, values, sinks):
    weights, rows, inv = [], [], []
    for (s_c, s_l), (vcm, vlm), sink in zip(scores, values, sinks):
        mx = jnp.maximum(jnp.max(s_c, axis=-1, keepdims=True), sink)
        if s_l is not None:
            mx = jnp.maximum(mx, jnp.max(s_l, axis=-1, keepdims=True))
        e_c = jnp.exp2(s_c - mx)
        den = jnp.exp2(sink - mx) + jnp.sum(e_c, axis=-1, keepdims=True)
        weights.append(e_c.astype(BF16))
        rows.append(vcm)
        if s_l is not None:
            e_l = jnp.exp2(s_l - mx)
            den = den + jnp.sum(e_l, axis=-1, keepdims=True)
            weights.append(e_l.astype(BF16))
            rows.append(vlm)
        inv.append(1.0 / den)
    out = _dot(jnp.concatenate(weights, axis=1), jnp.concatenate(rows, axis=0))
    low = lax.broadcasted_iota(jnp.int32, (1, LANES), 1) < HEAD_DIM
    return out * jnp.where(low, inv[0], inv[1])


def _mixer_kernel(*refs, n, tm, has_local, final_norm, attn_w, conv_w, pool_w, d_ff):
    it = iter(refs)
    x_ref, q_ref = next(it), next(it)
    if has_local:
        kk_ref, vv_ref = next(it), next(it)
    kc_ref, vc_ref = next(it), next(it)
    h_ref, p_ref, mod_ref, sink_ref = (next(it) for _ in range(4))
    dw_ref, cvec_ref, band_ref, poolw_ref, wout_ref, g2_ref, wfi_ref, wfo_ref, gf_ref = (next(it) for _ in range(9))
    out_ref = next(it)
    mix_ref, hp_ref, hs_ref, y_ref, y2_ref = (next(it) for _ in range(5))

    s = pl.program_id(0)
    tiles = pl.num_programs(0) - 1
    tiles_per_seq = n // tm
    i = lax.rem(jnp.minimum(s, tiles - 1), tiles_per_seq)
    last_i = tiles_per_seq - 1
    t0 = pl.multiple_of(i * tm, tm)

    @pl.when(s == 0)
    def _():
        mix_ref[...] = jnp.zeros_like(mix_ref)

    ffn_state = {}

    proj_rows = [slice(r, r + tm // 2) for r in range(0, tm, tm // 2)]

    def out_project():
        ffn_state["o"] = [_dot(mix_ref[rs, :], wout_ref[:, 0:x_ref.shape[2]]) for rs in proj_rows]

    def ffn_norm():
        gain = g2_ref[...] * (1.0 + mod_ref[0, 4:5, :])
        x1 = []
        for rs, o in zip(proj_rows, ffn_state["o"]):
            x1.append(x_ref[0, rs, :] + mod_ref[0, 2:3, :] * o)
            y2_ref[rs, :] = (_rms_norm(x1[-1], gain) + mod_ref[0, 3:4, :]).astype(BF16)
        ffn_state["x1"] = jnp.concatenate(x1, axis=0)

    ffn_parts = []
    deferred = []
    tokens = []

    def ffn_stages(a, w):
        state = {}

        def hidden():
            if tokens:
                corner = (slice(0, 2 * SUBLANES), slice(0, LANES))
                y2_ref[corner] = y2_ref[corner] + sum(tokens).astype(BF16)
                tokens.clear()
            lhs = y2_ref[...]
            state["gate"] = _dot(lhs, wfi_ref[:, a:a + w])
            state["up"] = _dot(lhs, wfi_ref[:, d_ff + a:d_ff + a + w])

        def project():
            gate = state["gate"]
            act = (gate * jax.nn.sigmoid(gate) * state["up"]).astype(BF16)
            part = _dot(act, wfo_ref[a:a + w, 0:x_ref.shape[2]])
            ffn_parts[:] = [part if not ffn_parts else ffn_parts[0] + part]
        return hidden, project

    ffn_pieces = [ffn_stages(a, w) for a, w in _ffn_chunks(d_ff)]


    zero = jnp.zeros((), BF16)
    n_groups = kc_ref.shape[2] // (2 * LANES)
    heads_per_group = attn_w // HEAD_DIM // n_groups

    def attention_stages(blk, g, pr):
        state = {}
        rows = slice(blk * Q_BLOCK, (blk + 1) * Q_BLOCK)
        col = (g * heads_per_group // 2 + pr) * LANES
        head = g * heads_per_group + 2 * pr

        half_cols = [slice((2 * g + hh) * LANES, (2 * g + hh + 1) * LANES) for hh in range(2)]
        start = t0 + blk * Q_BLOCK
        s0 = pl.multiple_of(jnp.clip(start - WINDOW, 0, n - SPAN), Q_BLOCK) if has_local else None

        def scores():
            if has_local:
                r = lax.broadcasted_iota(jnp.int32, (Q_BLOCK, SPAN), 0)
                c = lax.broadcasted_iota(jnp.int32, (Q_BLOCK, SPAN), 1)
                dist = (start - s0) + r - c
                lmask = (dist >= -WINDOW) & (dist <= WINDOW)
            else:
                lmask = None
            keys = [(kc_ref[0, :, cs], kk_ref[0, pl.ds(s0, SPAN), cs] if has_local else None) for cs in half_cols]
            state["scores"] = _attention_scores(q_ref[0, rows, col:col + LANES], keys, lmask)

        def output():
            values = [(vc_ref[0, :, cs], vv_ref[0, pl.ds(s0, SPAN), cs] if has_local else None) for cs in half_cols]
            out = _attention_output(state["scores"], values, (sink_ref[head] * LOG2E, sink_ref[head + 1] * LOG2E))
            deferred.append((mix_ref, (rows, slice(col, col + LANES)), out.astype(BF16)))
        return scores, output

    attention_pieces = [attention_stages(blk, g, pr) for blk in range(tm // Q_BLOCK)
                        for g in range(n_groups) for pr in range(heads_per_group // 2)]

    dw_b, ln_g, ln_b, pool_scale = (cvec_ref[k:k + 1, :] for k in range(4))

    def conv_prepare():
        hp_ref[CONV_HALO:CONV_HALO + tm, :] = h_ref[0, pl.ds(t0, tm), :].astype(F32)
        prev = h_ref[0, pl.ds(pl.multiple_of(jnp.maximum(t0 - CONV_HALO, 0), CONV_HALO), CONV_HALO), :]
        hp_ref[0:CONV_HALO, :] = jnp.where(i > 0, prev.astype(F32), 0.0)
        nxt = h_ref[0, pl.ds(pl.multiple_of(jnp.minimum(t0 + tm, n - CONV_HALO), CONV_HALO), CONV_HALO), :]
        hp_ref[CONV_HALO + tm:, :] = jnp.where(i < last_i, nxt.astype(F32), 0.0)

    def conv_piece(r0):
        def run():
            for sh in range(1, SUBLANES):
                hs_ref[sh - 1] = hp_ref[r0 + sh:r0 + sh + hs_ref.shape[1], :]
            acc = jnp.zeros((ROW_CHUNK, conv_w), F32)
            for k in range(CONV_KERNEL):
                base = CONV_HALO - CONV_PAD + k
                sh = base % SUBLANES
                if sh == 0:
                    tap = hp_ref[r0 + base:r0 + base + ROW_CHUNK, :]
                else:
                    tap = hs_ref[sh - 1, base - sh:base - sh + ROW_CHUNK, :]
                acc = acc + tap * dw_ref[k:k + 1, :]
            hc = acc + dw_b
            mu = jnp.mean(hc, axis=-1, keepdims=True)
            cen = hc - mu
            var = jnp.mean(cen * cen, axis=-1, keepdims=True)
            hn = cen * lax.rsqrt(var + EPS) * ln_g + ln_b
            res = hn * jax.nn.sigmoid(hn)
            deferred.append((mix_ref, (slice(r0, r0 + ROW_CHUNK), slice(attn_w, attn_w + conv_w)), res.astype(BF16)))
            tokens.append(_order_token(res))
        return run

    group_w = pool_w // len(POOL_WINDOWS)

    def pool_block(blk):
        def run():
            start = t0 + blk * Q_BLOCK
            s0 = pl.multiple_of(jnp.clip(start - POOL_HALO, 0, n - POOL_SPAN), POOL_HALO)
            which = (start - s0) // POOL_HALO
            span = p_ref[0, pl.ds(s0, POOL_SPAN), :]
            lane = lax.broadcasted_iota(jnp.int32, (1, pool_w), 1)
            y = None
            for gi in range(len(POOL_WINDOWS)):
                in_group = (lane >= gi * group_w) & (lane < (gi + 1) * group_w)
                part = _dot(band_ref[which, gi], jnp.where(in_group, span, zero))
                y = part if y is None else y + part
            y_ref[blk * Q_BLOCK:(blk + 1) * Q_BLOCK, :] = y.astype(BF16)
        return run

    def pool_project():
        pooled = _dot(y_ref[...], poolw_ref[...]) * pool_scale
        deferred.append((mix_ref, (slice(None), slice(attn_w + conv_w, None)), pooled.astype(BF16)))

    n_chunks = len(ffn_pieces)
    per_group = -(-len(attention_pieces) // n_chunks)
    groups = [attention_pieces[j * per_group:(j + 1) * per_group] for j in range(n_chunks)]
    score_stage = lambda j: [scores for scores, _ in groups[j]]
    output_stage = lambda j: [output for _, output in groups[j]]
    hidden_stage = lambda j: [ffn_pieces[j][0]]
    project_stage = lambda j: [ffn_pieces[j][1]]
    elementwise = ([conv_prepare] + [conv_piece(r0) for r0 in range(0, tm, ROW_CHUNK)]
                   + [pool_block(blk) for blk in range(tm // Q_BLOCK)])
    per_elem = -(-len(elementwise) // n_chunks)
    order = [out_project, ffn_norm] + score_stage(0)
    for j in range(n_chunks):
        order += hidden_stage(j) + elementwise[j * per_elem:(j + 1) * per_elem] + output_stage(j)
        if j + 1 < n_chunks:
            order += score_stage(j + 1)
        order += project_stage(j)
    order += [pool_project]
    for stage in order:
        stage()

    x2 = ffn_state["x1"] + mod_ref[0, 5:6, :] * ffn_parts[0]
    if final_norm:
        x2 = _rms_norm(x2, gf_ref[...])
    out_ref[0] = x2
    for ref, index, value in deferred:
        ref[index] = value


def _mixer(x, q, kk, vv, kc, vc, h, p, mod, sink, dw, cvec, band, poolw, wout, g2, wfi, wfo, gf,
           *, tm, final_norm):
    b, n, d = x.shape
    has_local = kk is not None
    attn_w, kv_w, conv_w, pool_w = q.shape[2], kc.shape[2], h.shape[2], p.shape[2]
    n_ctx = kc.shape[1]
    d_ff = wfo.shape[0]
    tiles_per_seq = n // tm
    tiles = b * tiles_per_seq
    mix_tile = lambda s: jnp.minimum(s, tiles - 1)
    ffn_tile = lambda s: jnp.maximum(s - 1, 0)
    const = lambda shape: pl.BlockSpec(shape, lambda s: (0,) * len(shape), pipeline_mode=pl.Buffered(1))
    tok = lambda width, tile: pl.BlockSpec(
        (1, tm, width), lambda s: (tile(s) // tiles_per_seq, tile(s) % tiles_per_seq, 0))
    seq = lambda length, width, tile: pl.BlockSpec((1, length, width), lambda s: (tile(s) // tiles_per_seq, 0, 0))
    in_specs = [tok(d, ffn_tile), tok(attn_w, mix_tile)]
    args = [x, q]
    if has_local:
        in_specs += [seq(n, kv_w, mix_tile), seq(n, kv_w, mix_tile)]
        args += [kk, vv]
    in_specs += [seq(n_ctx, kv_w, mix_tile), seq(n_ctx, kv_w, mix_tile), seq(n, conv_w, mix_tile),
                 seq(n, pool_w, mix_tile), seq(6, d, ffn_tile),
                 pl.BlockSpec(memory_space=pltpu.SMEM),
                 const(dw.shape), const(cvec.shape), const(band.shape), const(poolw.shape), const(wout.shape),
                 const(g2.shape), const(wfi.shape), const(wfo.shape), const(gf.shape)]
    args += [kc, vc, h, p, mod, sink, dw, cvec, band, poolw, wout, g2, wfi, wfo, gf]
    kern = functools.partial(_mixer_kernel, n=n, tm=tm, has_local=has_local, final_norm=final_norm,
                             attn_w=attn_w, conv_w=conv_w, pool_w=pool_w, d_ff=d_ff)
    return pl.pallas_call(
        kern, grid=(tiles + 1,), in_specs=in_specs, out_specs=tok(d, ffn_tile),
        out_shape=jax.ShapeDtypeStruct((b, n, d), F32),
        scratch_shapes=[
            pltpu.VMEM((tm, attn_w + conv_w + pool_w), BF16),
            pltpu.VMEM((tm + 2 * CONV_HALO, conv_w), F32),
            pltpu.VMEM((SUBLANES - 1, ROW_CHUNK + 2 * CONV_HALO - SUBLANES, conv_w), F32),
            pltpu.VMEM((tm, pool_w), BF16),
            pltpu.VMEM((tm, d), BF16),
        ],
        compiler_params=pltpu.CompilerParams(
            dimension_semantics=("arbitrary",), vmem_limit_bytes=VMEM_LIMIT_BYTES),
        name="mixer_local" if has_local else "mixer_ctx",
    )(*args)


def _rope_tables(n):
    rows = n // GRID_W
    row = jnp.repeat(jnp.arange(rows), GRID_W).astype(F32)
    col = jnp.tile(jnp.arange(GRID_W), rows).astype(F32)
    half = HEAD_DIM // 2
    inv = ROPE_BASE ** (-jnp.arange(0, half, 2, dtype=F32) / half)
    ar = row[:, None] * inv
    ac = col[:, None] * inv
    ang = jnp.concatenate([ar, ar, ac, ac], axis=-1)
    cos, sin = jnp.cos(ang), jnp.sin(ang)
    first = (jnp.arange(HEAD_DIM) // (HEAD_DIM // 4)) % 2 == 0
    sin_a = jnp.where(first, -sin, 0.0)
    sin_b = jnp.where(first, 0.0, sin)
    rep = LANES // HEAD_DIM
    return tuple(jnp.tile(t, (1, rep)) for t in (cos, sin_a, sin_b))


def _pool_bands(n):
    assert n >= POOL_SPAN and n % Q_BLOCK == 0
    r = jnp.arange(Q_BLOCK)[:, None]
    pos = jnp.arange(POOL_SPAN)[None, :]
    variants = []
    for k in range(3):
        t = k * POOL_HALO + r
        first = 0 if k == 0 else -POOL_SPAN
        last = POOL_SPAN - 1 if k == 2 else 2 * POOL_SPAN
        per_window = []
        for win in POOL_WINDOWS:
            lo = jnp.maximum(t - win // 2, first)
            hi = jnp.minimum(t + win - 1 - win // 2, last)
            inside = (pos >= lo) & (pos <= hi)
            mean = jnp.where(inside, 1.0 / (hi - lo + 1).astype(F32), 0.0)
            per_window.append(mean - (pos == t).astype(F32))
        variants.append(jnp.stack(per_window))
    return jnp.stack(variants).astype(BF16)


def _pad_lanes(w):
    return jnp.pad(w, ((0, 0), (0, LANES)))


def kernel(x, c, ctx, c_ctx, w_mod, b_mod, norm1_g, norm2_g, w_in, conv_dw, conv_dw_b, conv_ln_g, conv_ln_b,
           attn_sink, pool_w, pool_scale, w_out, w_ffn_in, w_ffn_out, final_g):
    b, n, d = x.shape
    n_ctx = ctx.shape[1]
    depth = w_mod.shape[0]
    heads = attn_sink.shape[1]
    attn_w = heads * HEAD_DIM
    conv_w = conv_dw.shape[2]
    pool_wd = pool_scale.shape[1]
    kv_w = w_in.shape[2] - attn_w - 2 * conv_w - pool_wd
    kv_w //= 2
    assert kv_w % LANES == 0
    dims = (attn_w, kv_w, conv_w, pool_wd)

    rows = -(-(b + 1) // SUBLANES) * SUBLANES
    cc = jnp.zeros((rows, d), F32).at[:b].set(c).at[b].set(c_ctx)
    mod_all = _modulation(cc, w_mod, b_mod)

    tables = _rope_tables(n)
    bands, bands_ctx = _pool_bands(n), _pool_bands(n_ctx)
    tm_in = min(1024, n)
    tm_mix = min(512, n)
    cx = ctx
    for l in range(depth):
        last = l == depth - 1
        mod = mod_all[l, :b].reshape(b, 6, d)
        mod_c = jnp.broadcast_to(mod_all[l, b].reshape(1, 6, d), (b, 6, d))
        w_all = w_in[l].astype(BF16)
        w_kv = w_all[:, attn_w:attn_w + 2 * kv_w]
        g1 = norm1_g[l].reshape(1, d)
        g2 = norm2_g[l].reshape(1, d)
        cvec = jnp.zeros((SUBLANES, conv_w), F32).at[0].set(conv_dw_b[l]).at[1].set(conv_ln_g[l]) \
            .at[2].set(conv_ln_b[l]).at[3].set(pool_scale[l])
        poolw = jax.scipy.linalg.block_diag(*[pool_w[l, gi] for gi in range(pool_w.shape[1])]).astype(BF16)
        head = (attn_sink[l], conv_dw[l], cvec)
        shared = (poolw, _pad_lanes(w_out[l].astype(BF16)), g2,
                  w_ffn_in[l].astype(BF16), _pad_lanes(w_ffn_out[l].astype(BF16)), final_g.reshape(1, d))

        q, kk, vv, h, p = _inproj(x, mod, g1, w_all, tables, tm=tm_in, kv_only=False, dims=dims)
        if last:
            kc, vc = _inproj(cx, mod_c, g1, w_kv, None, tm=n_ctx, kv_only=True, dims=dims)
        else:
            qc, kc, vc, hc, pc = _inproj(cx, mod_c, g1, w_all, None, tm=n_ctx, kv_only=False, dims=dims)
        x = _mixer(x, q, kk, vv, kc, vc, h, p, mod, *head, bands, *shared, tm=tm_mix, final_norm=last)
        if not last:
            cx = _mixer(cx, qc, None, None, kc, vc, hc, pc, mod_c, *head, bands_ctx, *shared,
                        tm=n_ctx, final_norm=False)
    return x
```

```python
import functools

import jax
import jax.numpy as jnp
from jax import lax
from jax.experimental import pallas as pl
from jax.experimental.pallas import tpu as pltpu

F32 = jnp.float32
BF16 = jnp.bfloat16

LANES = 128
SUBLANES = 8
MXU_COLS = 256
FFN_CHUNK_TILES = 3
VMEM_LIMIT_BYTES = 62 * 1024 * 1024

GRID_W = 64
HEAD_DIM = 64
ROPE_BASE = 10000.0
WINDOW = 128
Q_BLOCK = 128
SPAN = Q_BLOCK + 2 * WINDOW
CONV_KERNEL = 31
CONV_PAD = CONV_KERNEL // 2
CONV_HALO = 16
POOL_WINDOWS = (2, 4, 8, 16)
POOL_HALO = 16
POOL_SPAN = Q_BLOCK + 2 * POOL_HALO
ROW_CHUNK = 64
INPROJ_ROW_GROUPS = 1
EPS = 1e-6
NEG = -1e30
LOG2E = 1.4426950408889634


def _rms_norm(x, g):
    return x * lax.rsqrt(jnp.mean(x * x, axis=-1, keepdims=True) + EPS) * g


_dot = functools.partial(jnp.dot, preferred_element_type=F32)
_dot_nt = functools.partial(lax.dot_general, dimension_numbers=(((1,), (1,)), ((), ())), preferred_element_type=F32)


def _mod_kernel(c_ref, w_ref, b_ref, o_ref):
    c = c_ref[...]
    a = c * jax.nn.sigmoid(c)
    o_ref[0] = jnp.dot(a, w_ref[0], preferred_element_type=F32,
                       precision=lax.Precision.HIGHEST) + b_ref[0]


def _modulation(cc, w_mod, b_mod):
    depth, d, d6 = w_mod.shape
    rows = cc.shape[0]
    tn = d6 // 4
    return pl.pallas_call(
        _mod_kernel,
        grid=(depth, d6 // tn),
        in_specs=[
            pl.BlockSpec((rows, d), lambda l, j: (0, 0)),
            pl.BlockSpec((1, d, tn), lambda l, j: (l, 0, j)),
            pl.BlockSpec((1, 1, tn), lambda l, j: (l, 0, j)),
        ],
        out_specs=pl.BlockSpec((1, rows, tn), lambda l, j: (l, 0, j)),
        out_shape=jax.ShapeDtypeStruct((depth, rows, d6), F32),
        compiler_params=pltpu.CompilerParams(
            dimension_semantics=("arbitrary", "arbitrary"), vmem_limit_bytes=VMEM_LIMIT_BYTES),
        name="modulation",
    )(cc, w_mod, b_mod.reshape(depth, 1, d6))


def _rope(t, cos, sin_a, sin_b):
    quarter = HEAD_DIM // 4
    return (t * cos + pltpu.roll(t, LANES - quarter, 1) * sin_a + pltpu.roll(t, quarter, 1) * sin_b)


def _inproj_kernel(*refs, rope, kv_only, attn_w, kv_w, conv_w):
    if rope:
        x_ref, mod_ref, g_ref, w_ref, cos_ref, sa_ref, sb_ref = refs[:7]
        outs = refs[7:]
    else:
        x_ref, mod_ref, g_ref, w_ref = refs[:4]
        outs = refs[4:]
    tm = x_ref.shape[1]
    groups = [slice(r, r + tm // INPROJ_ROW_GROUPS) for r in range(0, tm, tm // INPROJ_ROW_GROUPS)]
    gain = g_ref[...] * (1.0 + mod_ref[0, 1:2, :])
    projected = []
    for rs in groups:
        hl = _rms_norm(x_ref[0, rs, :], gain) + mod_ref[0, 0:1, :]
        projected.append(_dot(hl.astype(BF16), w_ref[...]))
    low = lax.broadcasted_iota(jnp.int32, (1, LANES), 1) < HEAD_DIM
    high = jnp.logical_not(low)
    for rs, u in zip(groups, projected):
        def maybe_rope(t):
            if rope:
                return _rope(t, cos_ref[rs, :], sa_ref[rs, :], sb_ref[rs, :])
            return t

        if kv_only:
            kk_ref, vv_ref = outs
            off = 0
        else:
            q_ref, kk_ref, vv_ref, h_ref, p_ref = outs
            for c in range(attn_w // LANES):
                t = maybe_rope(u[:, c * LANES:(c + 1) * LANES])
                q_ref[0, rs, c * LANES:(c + 1) * LANES] = (t * (HEAD_DIM ** -0.5 * LOG2E)).astype(BF16)
            off = attn_w
        for c in range(kv_w // LANES):
            k = maybe_rope(u[:, off + c * LANES: off + (c + 1) * LANES])
            v = u[:, off + kv_w + c * LANES: off + kv_w + (c + 1) * LANES]
            for t, t_ref in ((k, kk_ref), (v, vv_ref)):
                swapped = pltpu.roll(t, HEAD_DIM, 1)
                for j, (keep, src) in enumerate(((low, t), (high, swapped), (low, swapped), (high, t))):
                    cols = slice((4 * c + j) * LANES, (4 * c + j + 1) * LANES)
                    t_ref[0, rs, cols] = jnp.where(keep, src, 0.0).astype(BF16)
        off += 2 * kv_w
        if not kv_only:
            a = u[:, off:off + conv_w]
            g = u[:, off + conv_w:off + 2 * conv_w]
            h_ref[0, rs, :] = (a * jax.nn.sigmoid(g)).astype(BF16)
            off += 2 * conv_w
            p_ref[0, rs, :] = u[:, off:].astype(BF16)


def _inproj(x, mod, g, w, tables, *, tm, kv_only, dims):
    b, n, d = x.shape
    attn_w, kv_w, conv_w, pool_w = dims
    rope = tables is not None
    grid = (n // tm, b)
    in_specs = [
        pl.BlockSpec((1, tm, d), lambda i, bb: (bb, i, 0)),
        pl.BlockSpec((1, 6, d), lambda i, bb: (bb, 0, 0)),
        pl.BlockSpec((1, d), lambda i, bb: (0, 0)),
        pl.BlockSpec(w.shape, lambda i, bb: (0, 0)),
    ]
    args = [x, mod, g, w]
    if rope:
        in_specs += [pl.BlockSpec((tm, LANES), lambda i, bb: (i, 0))] * 3
        args += list(tables)
    tok = lambda width: pl.BlockSpec((1, tm, width), lambda i, bb: (bb, i, 0))
    out_specs = [tok(4 * kv_w), tok(4 * kv_w)]
    out_shape = [jax.ShapeDtypeStruct((b, n, 4 * kv_w), BF16)] * 2
    if not kv_only:
        out_specs = [tok(attn_w)] + out_specs + [tok(conv_w), tok(pool_w)]
        out_shape = ([jax.ShapeDtypeStruct((b, n, attn_w), BF16)] + out_shape
                     + [jax.ShapeDtypeStruct((b, n, conv_w), BF16), jax.ShapeDtypeStruct((b, n, pool_w), BF16)])
    kern = functools.partial(_inproj_kernel, rope=rope, kv_only=kv_only,
                             attn_w=attn_w, kv_w=kv_w, conv_w=conv_w)
    return pl.pallas_call(
        kern, grid=grid, in_specs=in_specs, out_specs=out_specs, out_shape=out_shape,
        compiler_params=pltpu.CompilerParams(
            dimension_semantics=("arbitrary", "arbitrary"), vmem_limit_bytes=VMEM_LIMIT_BYTES),
        name="inproj_kv" if kv_only else "inproj",
    )(*args)


def _ffn_tile(d_ff):
    return MXU_COLS if d_ff % MXU_COLS == 0 else LANES


def _ffn_chunks(d_ff):
    unit = _ffn_tile(d_ff)
    units = d_ff // unit
    per = FFN_CHUNK_TILES
    return [(u * unit, min(per, units - u) * unit) for u in range(0, units, per)]


def _order_token(v):
    return (v[0:2 * SUBLANES, 0:LANES] > 0).astype(F32) * 0.0


def _attention_scores(q2, keys, lmask):
    rows = [k for pair in keys for k in pair if k is not None]
    s = _dot_nt(q2, jnp.concatenate(rows, axis=0))
    scores, off = [], 0
    for kcm, klm in keys:
        s_c = s[:, off:off + kcm.shape[0]]
        off += kcm.shape[0]
        s_l = None
        if klm is not None:
            s_l = jnp.where(lmask, s[:, off:off + klm.shape[0]], NEG)
            off += klm.shape[0]
        scores.append((s_c, s_l))
    return scores


def _attention_output(scores, values, sinks):
    weights, rows, inv = [], [], []
    for (s_c, s_l), (vcm, vlm), sink in zip(scores, values, sinks):
        mx = jnp.maximum(jnp.max(s_c, axis=-1, keepdims=True), sink)
        if s_l is not None:
            mx = jnp.maximum(mx, jnp.max(s_l, axis=-1, keepdims=True))
        e_c = jnp.exp2(s_c - mx)
        den = jnp.exp2(sink - mx) + jnp.sum(e_c, axis=-1, keepdims=True)
        weights.append(e_c.astype(BF16))
        rows.append(vcm)
        if s_l is not None:
            e_l = jnp.exp2(s_l - mx)
            den = den + jnp.sum(e_l, axis=-1, keepdims=True)
            weights.append(e_l.astype(BF16))
            rows.append(vlm)
        inv.append(1.0 / den)
    out = _dot(jnp.concatenate(weights, axis=1), jnp.concatenate(rows, axis=0))
    low = lax.broadcasted_iota(jnp.int32, (1, LANES), 1) < HEAD_DIM
    return out * jnp.where(low, inv[0], inv[1])


def _mixer_kernel(*refs, n, tm, has_local, final_norm, attn_w, conv_w, pool_w, d_ff):
    it = iter(refs)
    x_ref, q_ref = next(it), next(it)
    if has_local:
        kk_ref, vv_ref = next(it), next(it)
    kc_ref, vc_ref = next(it), next(it)
    h_ref, p_ref, mod_ref, sink_ref = (next(it) for _ in range(4))
    dw_ref, cvec_ref, band_ref, poolw_ref, wout_ref, g2_ref, wfi_ref, wfo_ref, gf_ref = (next(it) for _ in range(9))
    out_ref = next(it)
    mix_ref, hp_ref, hs_ref, y_ref, y2_ref = (next(it) for _ in range(5))

    s = pl.program_id(0)
    tiles = pl.num_programs(0) - 1
    tiles_per_seq = n // tm
    i = lax.rem(jnp.minimum(s, tiles - 1), tiles_per_seq)
    last_i = tiles_per_seq - 1
    t0 = pl.multiple_of(i * tm, tm)

    @pl.when(s == 0)
    def _():
        mix_ref[...] = jnp.zeros_like(mix_ref)

    ffn_state = {}

    def out_project():
        ffn_state["o"] = _dot(mix_ref[...], wout_ref[:, 0:x_ref.shape[2]])

    def ffn_norm():
        x1 = x_ref[0] + mod_ref[0, 2:3, :] * ffn_state["o"]
        y2 = _rms_norm(x1, g2_ref[...] * (1.0 + mod_ref[0, 4:5, :])) + mod_ref[0, 3:4, :]
        y2_ref[...] = y2.astype(BF16)
        ffn_state["x1"] = x1

    ffn_parts = []
    deferred = []
    tokens = []

    def ffn_stages(a, w):
        state = {}

        def hidden():
            if tokens:
                corner = (slice(0, 2 * SUBLANES), slice(0, LANES))
                y2_ref[corner] = y2_ref[corner] + sum(tokens).astype(BF16)
                tokens.clear()
            lhs = y2_ref[...]
            state["gate"] = _dot(lhs, wfi_ref[:, a:a + w])
            state["up"] = _dot(lhs, wfi_ref[:, d_ff + a:d_ff + a + w])

        def project():
            gate = state["gate"]
            act = (gate * jax.nn.sigmoid(gate) * state["up"]).astype(BF16)
            part = _dot(act, wfo_ref[a:a + w, 0:x_ref.shape[2]])
            ffn_parts[:] = [part if not ffn_parts else ffn_parts[0] + part]
        return hidden, project

    ffn_pieces = [ffn_stages(a, w) for a, w in _ffn_chunks(d_ff)]


    zero = jnp.zeros((), BF16)
    n_groups = kc_ref.shape[2] // (2 * LANES)
    heads_per_group = attn_w // HEAD_DIM // n_groups

    def attention_stages(blk, g, pr):
        state = {}
        rows = slice(blk * Q_BLOCK, (blk + 1) * Q_BLOCK)
        col = (g * heads_per_group // 2 + pr) * LANES
        head = g * heads_per_group + 2 * pr

        half_cols = [slice((2 * g + hh) * LANES, (2 * g + hh + 1) * LANES) for hh in range(2)]
        start = t0 + blk * Q_BLOCK
        s0 = pl.multiple_of(jnp.clip(start - WINDOW, 0, n - SPAN), Q_BLOCK) if has_local else None

        def scores():
            if has_local:
                r = lax.broadcasted_iota(jnp.int32, (Q_BLOCK, SPAN), 0)
                c = lax.broadcasted_iota(jnp.int32, (Q_BLOCK, SPAN), 1)
                dist = (start - s0) + r - c
                lmask = (dist >= -WINDOW) & (dist <= WINDOW)
            else:
                lmask = None
            keys = [(kc_ref[0, :, cs], kk_ref[0, pl.ds(s0, SPAN), cs] if has_local else None) for cs in half_cols]
            state["scores"] = _attention_scores(q_ref[0, rows, col:col + LANES], keys, lmask)

        def output():
            values = [(vc_ref[0, :, cs], vv_ref[0, pl.ds(s0, SPAN), cs] if has_local else None) for cs in half_cols]
            out = _attention_output(state["scores"], values, (sink_ref[head] * LOG2E, sink_ref[head + 1] * LOG2E))
            deferred.append((mix_ref, (rows, slice(col, col + LANES)), out.astype(BF16)))
        return scores, output

    attention_pieces = [attention_stages(blk, g, pr) for blk in range(tm // Q_BLOCK)
                        for g in range(n_groups) for pr in range(heads_per_group // 2)]

    dw_b, ln_g, ln_b, pool_scale = (cvec_ref[k:k + 1, :] for k in range(4))

    def conv_prepare():
        hp_ref[CONV_HALO:CONV_HALO + tm, :] = h_ref[0, pl.ds(t0, tm), :].astype(F32)
        prev = h_ref[0, pl.ds(pl.multiple_of(jnp.maximum(t0 - CONV_HALO, 0), CONV_HALO), CONV_HALO), :]
        hp_ref[0:CONV_HALO, :] = jnp.where(i > 0, prev.astype(F32), 0.0)
        nxt = h_ref[0, pl.ds(pl.multiple_of(jnp.minimum(t0 + tm, n - CONV_HALO), CONV_HALO), CONV_HALO), :]
        hp_ref[CONV_HALO + tm:, :] = jnp.where(i < last_i, nxt.astype(F32), 0.0)

    def conv_piece(r0):
        def run():
            for sh in range(1, SUBLANES):
                hs_ref[sh - 1] = hp_ref[r0 + sh:r0 + sh + hs_ref.shape[1], :]
            acc = jnp.zeros((ROW_CHUNK, conv_w), F32)
            for k in range(CONV_KERNEL):
                base = CONV_HALO - CONV_PAD + k
                sh = base % SUBLANES
                if sh == 0:
                    tap = hp_ref[r0 + base:r0 + base + ROW_CHUNK, :]
                else:
                    tap = hs_ref[sh - 1, base - sh:base - sh + ROW_CHUNK, :]
                acc = acc + tap * dw_ref[k:k + 1, :]
            hc = acc + dw_b
            mu = jnp.mean(hc, axis=-1, keepdims=True)
            cen = hc - mu
            var = jnp.mean(cen * cen, axis=-1, keepdims=True)
            hn = cen * lax.rsqrt(var + EPS) * ln_g + ln_b
            res = hn * jax.nn.sigmoid(hn)
            deferred.append((mix_ref, (slice(r0, r0 + ROW_CHUNK), slice(attn_w, attn_w + conv_w)), res.astype(BF16)))
            tokens.append(_order_token(res))
        return run

    group_w = pool_w // len(POOL_WINDOWS)

    def pool_block(blk):
        def run():
            start = t0 + blk * Q_BLOCK
            s0 = pl.multiple_of(jnp.clip(start - POOL_HALO, 0, n - POOL_SPAN), POOL_HALO)
            which = (start - s0) // POOL_HALO
            span = p_ref[0, pl.ds(s0, POOL_SPAN), :]
            lane = lax.broadcasted_iota(jnp.int32, (1, pool_w), 1)
            y = None
            for gi in range(len(POOL_WINDOWS)):
                in_group = (lane >= gi * group_w) & (lane < (gi + 1) * group_w)
                part = _dot(band_ref[which, gi], jnp.where(in_group, span, zero))
                y = part if y is None else y + part
            y_ref[blk * Q_BLOCK:(blk + 1) * Q_BLOCK, :] = y.astype(BF16)
        return run

    def pool_project():
        pooled = _dot(y_ref[...], poolw_ref[...]) * pool_scale
        deferred.append((mix_ref, (slice(None), slice(attn_w + conv_w, None)), pooled.astype(BF16)))

    n_chunks = len(ffn_pieces)
    per_group = -(-len(attention_pieces) // n_chunks)
    groups = [attention_pieces[j * per_group:(j + 1) * per_group] for j in range(n_chunks)]
    score_stage = lambda j: [scores for scores, _ in groups[j]]
    output_stage = lambda j: [output for _, output in groups[j]]
    hidden_stage = lambda j: [ffn_pieces[j][0]]
    project_stage = lambda j: [ffn_pieces[j][1]]
    elementwise = ([conv_prepare] + [conv_piece(r0) for r0 in range(0, tm, ROW_CHUNK)]
                   + [pool_block(blk) for blk in range(tm // Q_BLOCK)])
    per_elem = -(-len(elementwise) // n_chunks)
    order = [out_project, ffn_norm] + score_stage(0)
    for j in range(n_chunks):
        order += hidden_stage(j) + elementwise[j * per_elem:(j + 1) * per_elem] + output_stage(j)
        if j + 1 < n_chunks:
            order += score_stage(j + 1)
        order += project_stage(j)
    order += [pool_project]
    for stage in order:
        stage()

    x2 = ffn_state["x1"] + mod_ref[0, 5:6, :] * ffn_parts[0]
    if final_norm:
        x2 = _rms_norm(x2, gf_ref[...])
    out_ref[0] = x2
    for ref, index, value in deferred:
        ref[index] = value


def _mixer(x, q, kk, vv, kc, vc, h, p, mod, sink, dw, cvec, band, poolw, wout, g2, wfi, wfo, gf,
           *, tm, final_norm):
    b, n, d = x.shape
    has_local = kk is not None
    attn_w, kv_w, conv_w, pool_w = q.shape[2], kc.shape[2], h.shape[2], p.shape[2]
    n_ctx = kc.shape[1]
    d_ff = wfo.shape[0]
    tiles_per_seq = n // tm
    tiles = b * tiles_per_seq
    mix_tile = lambda s: jnp.minimum(s, tiles - 1)
    ffn_tile = lambda s: jnp.maximum(s - 1, 0)
    const = lambda shape: pl.BlockSpec(shape, lambda s: (0,) * len(shape), pipeline_mode=pl.Buffered(1))
    tok = lambda width, tile: pl.BlockSpec(
        (1, tm, width), lambda s: (tile(s) // tiles_per_seq, tile(s) % tiles_per_seq, 0))
    seq = lambda length, width, tile: pl.BlockSpec((1, length, width), lambda s: (tile(s) // tiles_per_seq, 0, 0))
    in_specs = [tok(d, ffn_tile), tok(attn_w, mix_tile)]
    args = [x, q]
    if has_local:
        in_specs += [seq(n, kv_w, mix_tile), seq(n, kv_w, mix_tile)]
        args += [kk, vv]
    in_specs += [seq(n_ctx, kv_w, mix_tile), seq(n_ctx, kv_w, mix_tile), seq(n, conv_w, mix_tile),
                 seq(n, pool_w, mix_tile), seq(6, d, ffn_tile),
                 pl.BlockSpec(memory_space=pltpu.SMEM),
                 const(dw.shape), const(cvec.shape), const(band.shape), const(poolw.shape), const(wout.shape),
                 const(g2.shape), const(wfi.shape), const(wfo.shape), const(gf.shape)]
    args += [kc, vc, h, p, mod, sink, dw, cvec, band, poolw, wout, g2, wfi, wfo, gf]
    kern = functools.partial(_mixer_kernel, n=n, tm=tm, has_local=has_local, final_norm=final_norm,
                             attn_w=attn_w, conv_w=conv_w, pool_w=pool_w, d_ff=d_ff)
    return pl.pallas_call(
        kern, grid=(tiles + 1,), in_specs=in_specs, out_specs=tok(d, ffn_tile),
        out_shape=jax.ShapeDtypeStruct((b, n, d), F32),
        scratch_shapes=[
            pltpu.VMEM((tm, attn_w + conv_w + pool_w), BF16),
            pltpu.VMEM((tm + 2 * CONV_HALO, conv_w), F32),
            pltpu.VMEM((SUBLANES - 1, ROW_CHUNK + 2 * CONV_HALO - SUBLANES, conv_w), F32),
            pltpu.VMEM((tm, pool_w), BF16),
            pltpu.VMEM((tm, d), BF16),
        ],
        compiler_params=pltpu.CompilerParams(
            dimension_semantics=("arbitrary",), vmem_limit_bytes=VMEM_LIMIT_BYTES),
        name="mixer_local" if has_local else "mixer_ctx",
    )(*args)


def _rope_tables(n):
    rows = n // GRID_W
    row = jnp.repeat(jnp.arange(rows), GRID_W).astype(F32)
    col = jnp.tile(jnp.arange(GRID_W), rows).astype(F32)
    half = HEAD_DIM // 2
    inv = ROPE_BASE ** (-jnp.arange(0, half, 2, dtype=F32) / half)
    ar = row[:, None] * inv
    ac = col[:, None] * inv
    ang = jnp.concatenate([ar, ar, ac, ac], axis=-1)
    cos, sin = jnp.cos(ang), jnp.sin(ang)
    first = (jnp.arange(HEAD_DIM) // (HEAD_DIM // 4)) % 2 == 0
    sin_a = jnp.where(first, -sin, 0.0)
    sin_b = jnp.where(first, 0.0, sin)
    rep = LANES // HEAD_DIM
    return tuple(jnp.tile(t, (1, rep)) for t in (cos, sin_a, sin_b))


def _pool_bands(n):
    assert n >= POOL_SPAN and n % Q_BLOCK == 0
    r = jnp.arange(Q_BLOCK)[:, None]
    pos = jnp.arange(POOL_SPAN)[None, :]
    variants = []
    for k in range(3):
        t = k * POOL_HALO + r
        first = 0 if k == 0 else -POOL_SPAN
        last = POOL_SPAN - 1 if k == 2 else 2 * POOL_SPAN
        per_window = []
        for win in POOL_WINDOWS:
            lo = jnp.maximum(t - win // 2, first)
            hi = jnp.minimum(t + win - 1 - win // 2, last)
            inside = (pos >= lo) & (pos <= hi)
            mean = jnp.where(inside, 1.0 / (hi - lo + 1).astype(F32), 0.0)
            per_window.append(mean - (pos == t).astype(F32))
        variants.append(jnp.stack(per_window))
    return jnp.stack(variants).astype(BF16)


def _pad_lanes(w):
    return jnp.pad(w, ((0, 0), (0, LANES)))


def kernel(x, c, ctx, c_ctx, w_mod, b_mod, norm1_g, norm2_g, w_in, conv_dw, conv_dw_b, conv_ln_g, conv_ln_b,
           attn_sink, pool_w, pool_scale, w_out, w_ffn_in, w_ffn_out, final_g):
    b, n, d = x.shape
    n_ctx = ctx.shape[1]
    depth = w_mod.shape[0]
    heads = attn_sink.shape[1]
    attn_w = heads * HEAD_DIM
    conv_w = conv_dw.shape[2]
    pool_wd = pool_scale.shape[1]
    kv_w = w_in.shape[2] - attn_w - 2 * conv_w - pool_wd
    kv_w //= 2
    assert kv_w % LANES == 0
    dims = (attn_w, kv_w, conv_w, pool_wd)

    rows = -(-(b + 1) // SUBLANES) * SUBLANES
    cc = jnp.zeros((rows, d), F32).at[:b].set(c).at[b].set(c_ctx)
    mod_all = _modulation(cc, w_mod, b_mod)

    tables = _rope_tables(n)
    bands, bands_ctx = _pool_bands(n), _pool_bands(n_ctx)
    tm_in = min(1024, n)
    tm_mix = min(512, n)
    ctx_rows = tm_in if (b * n_ctx) % tm_in == 0 and tm_in % n_ctx == 0 else n_ctx
    ctx_groups = b * n_ctx // ctx_rows
    cx = ctx
    for l in range(depth):
        last = l == depth - 1
        mod = mod_all[l, :b].reshape(b, 6, d)
        mod_c = jnp.broadcast_to(mod_all[l, b].reshape(1, 6, d), (b, 6, d))
        w_all = w_in[l].astype(BF16)
        w_kv = w_all[:, attn_w:attn_w + 2 * kv_w]
        g1 = norm1_g[l].reshape(1, d)
        g2 = norm2_g[l].reshape(1, d)
        cvec = jnp.zeros((SUBLANES, conv_w), F32).at[0].set(conv_dw_b[l]).at[1].set(conv_ln_g[l]) \
            .at[2].set(conv_ln_b[l]).at[3].set(pool_scale[l])
        poolw = jax.scipy.linalg.block_diag(*[pool_w[l, gi] for gi in range(pool_w.shape[1])]).astype(BF16)
        head = (attn_sink[l], conv_dw[l], cvec)
        shared = (poolw, _pad_lanes(w_out[l].astype(BF16)), g2,
                  w_ffn_in[l].astype(BF16), _pad_lanes(w_ffn_out[l].astype(BF16)), final_g.reshape(1, d))

        q, kk, vv, h, p = _inproj(x, mod, g1, w_all, tables, tm=tm_in, kv_only=False, dims=dims)
        ctx_outs = _inproj(cx.reshape(ctx_groups, ctx_rows, d), mod_c[:ctx_groups], g1, w_kv if last else w_all,
                           None, tm=ctx_rows, kv_only=last, dims=dims)
        ctx_outs = [t.reshape(b, n_ctx, t.shape[-1]) for t in ctx_outs]
        if last:
            kc, vc = ctx_outs
        else:
            qc, kc, vc, hc, pc = ctx_outs
        x = _mixer(x, q, kk, vv, kc, vc, h, p, mod, *head, bands, *shared, tm=tm_mix, final_norm=last)
        if not last:
            cx = _mixer(cx, qc, None, None, kc, vc, hc, pc, mod_c, *head, bands_ctx, *shared,
                        tm=n_ctx, final_norm=False)
    return x
```

```python
import functools

import jax
import jax.numpy as jnp
from jax import lax
from jax.experimental import pallas as pl
from jax.experimental.pallas import tpu as pltpu

F32 = jnp.float32
BF16 = jnp.bfloat16

LANES = 128
SUBLANES = 8
MXU_COLS = 256
FFN_CHUNK_TILES = 3
VMEM_LIMIT_BYTES = 62 * 1024 * 1024

GRID_W = 64
HEAD_DIM = 64
ROPE_BASE = 10000.0
WINDOW = 128
Q_BLOCK = 128
SPAN = Q_BLOCK + 2 * WINDOW
CONV_KERNEL = 31
CONV_PAD = CONV_KERNEL // 2
CONV_HALO = 16
POOL_WINDOWS = (2, 4, 8, 16)
POOL_HALO = 16
POOL_SPAN = Q_BLOCK + 2 * POOL_HALO
ROW_CHUNK = 64
INPROJ_ROW_GROUPS = 1
EPS = 1e-6
NEG = -1e30
LOG2E = 1.4426950408889634


def _rms_norm(x, g):
    return x * lax.rsqrt(jnp.mean(x * x, axis=-1, keepdims=True) + EPS) * g


_dot = functools.partial(jnp.dot, preferred_element_type=F32)
_dot_nt = functools.partial(lax.dot_general, dimension_numbers=(((1,), (1,)), ((), ())), preferred_element_type=F32)


def _mod_kernel(c_ref, w_ref, b_ref, o_ref):
    c = c_ref[...]
    a = c * jax.nn.sigmoid(c)
    o_ref[0] = jnp.dot(a, w_ref[0], preferred_element_type=F32,
                       precision=lax.Precision.HIGHEST) + b_ref[0]


def _modulation(cc, w_mod, b_mod):
    depth, d, d6 = w_mod.shape
    rows = cc.shape[0]
    tn = d6 // 4
    return pl.pallas_call(
        _mod_kernel,
        grid=(depth, d6 // tn),
        in_specs=[
            pl.BlockSpec((rows, d), lambda l, j: (0, 0)),
            pl.BlockSpec((1, d, tn), lambda l, j: (l, 0, j)),
            pl.BlockSpec((1, 1, tn), lambda l, j: (l, 0, j)),
        ],
        out_specs=pl.BlockSpec((1, rows, tn), lambda l, j: (l, 0, j)),
        out_shape=jax.ShapeDtypeStruct((depth, rows, d6), F32),
        compiler_params=pltpu.CompilerParams(
            dimension_semantics=("arbitrary", "arbitrary"), vmem_limit_bytes=VMEM_LIMIT_BYTES),
        name="modulation",
    )(cc, w_mod, b_mod.reshape(depth, 1, d6))


def _rope(t, cos, sin_a, sin_b):
    quarter = HEAD_DIM // 4
    return (t * cos + pltpu.roll(t, LANES - quarter, 1) * sin_a + pltpu.roll(t, quarter, 1) * sin_b)


def _inproj_kernel(*refs, rope, kv_only, attn_w, kv_w, conv_w):
    if rope:
        x_ref, mod_ref, g_ref, w_ref, cos_ref, sa_ref, sb_ref = refs[:7]
        outs = refs[7:]
    else:
        x_ref, mod_ref, g_ref, w_ref = refs[:4]
        outs = refs[4:]
    tm = x_ref.shape[1]
    groups = [slice(r, r + tm // INPROJ_ROW_GROUPS) for r in range(0, tm, tm // INPROJ_ROW_GROUPS)]
    gain = g_ref[...] * (1.0 + mod_ref[0, 1:2, :])
    projected = []
    for rs in groups:
        hl = _rms_norm(x_ref[0, rs, :], gain) + mod_ref[0, 0:1, :]
        projected.append(_dot(hl.astype(BF16), w_ref[...]))
    low = lax.broadcasted_iota(jnp.int32, (1, LANES), 1) < HEAD_DIM
    high = jnp.logical_not(low)
    for rs, u in zip(groups, projected):
        def maybe_rope(t):
            if rope:
                return _rope(t, cos_ref[rs, :], sa_ref[rs, :], sb_ref[rs, :])
            return t

        if kv_only:
            kk_ref, vv_ref = outs
            off = 0
        else:
            q_ref, kk_ref, vv_ref, h_ref, p_ref = outs
            for c in range(attn_w // LANES):
                t = maybe_rope(u[:, c * LANES:(c + 1) * LANES])
                q_ref[0, rs, c * LANES:(c + 1) * LANES] = (t * (HEAD_DIM ** -0.5 * LOG2E)).astype(BF16)
            off = attn_w
        for c in range(kv_w // LANES):
            k = maybe_rope(u[:, off + c * LANES: off + (c + 1) * LANES])
            v = u[:, off + kv_w + c * LANES: off + kv_w + (c + 1) * LANES]
            for t, t_ref in ((k, kk_ref), (v, vv_ref)):
                swapped = pltpu.roll(t, HEAD_DIM, 1)
                for j, (keep, src) in enumerate(((low, t), (high, swapped), (low, swapped), (high, t))):
                    cols = slice((4 * c + j) * LANES, (4 * c + j + 1) * LANES)
                    t_ref[0, rs, cols] = jnp.where(keep, src, 0.0).astype(BF16)
        off += 2 * kv_w
        if not kv_only:
            a = u[:, off:off + conv_w]
            g = u[:, off + conv_w:off + 2 * conv_w]
            h_ref[0, rs, :] = (a * jax.nn.sigmoid(g)).astype(BF16)
            off += 2 * conv_w
            p_ref[0, rs, :] = u[:, off:].astype(BF16)


def _inproj(x, mod, g, w, tables, *, tm, kv_only, dims):
    b, n, d = x.shape
    attn_w, kv_w, conv_w, pool_w = dims
    rope = tables is not None
    grid = (n // tm, b)
    in_specs = [
        pl.BlockSpec((1, tm, d), lambda i, bb: (bb, i, 0)),
        pl.BlockSpec((1, 6, d), lambda i, bb: (bb, 0, 0)),
        pl.BlockSpec((1, d), lambda i, bb: (0, 0)),
        pl.BlockSpec(w.shape, lambda i, bb: (0, 0)),
    ]
    args = [x, mod, g, w]
    if rope:
        in_specs += [pl.BlockSpec((tm, LANES), lambda i, bb: (i, 0))] * 3
        args += list(tables)
    tok = lambda width: pl.BlockSpec((1, tm, width), lambda i, bb: (bb, i, 0))
    out_specs = [tok(4 * kv_w), tok(4 * kv_w)]
    out_shape = [jax.ShapeDtypeStruct((b, n, 4 * kv_w), BF16)] * 2
    if not kv_only:
        out_specs = [tok(attn_w)] + out_specs + [tok(conv_w), tok(pool_w)]
        out_shape = ([jax.ShapeDtypeStruct((b, n, attn_w), BF16)] + out_shape
                     + [jax.ShapeDtypeStruct((b, n, conv_w), BF16), jax.ShapeDtypeStruct((b, n, pool_w), BF16)])
    kern = functools.partial(_inproj_kernel, rope=rope, kv_only=kv_only,
                             attn_w=attn_w, kv_w=kv_w, conv_w=conv_w)
    return pl.pallas_call(
        kern, grid=grid, in_specs=in_specs, out_specs=out_specs, out_shape=out_shape,
        compiler_params=pltpu.CompilerParams(
            dimension_semantics=("arbitrary", "arbitrary"), vmem_limit_bytes=VMEM_LIMIT_BYTES),
        name="inproj_kv" if kv_only else "inproj",
    )(*args)


def _ffn_tile(d_ff):
    return MXU_COLS if d_ff % MXU_COLS == 0 else LANES


def _ffn_chunks(d_ff):
    unit = _ffn_tile(d_ff)
    units = d_ff // unit
    per = FFN_CHUNK_TILES
    return [(u * unit, min(per, units - u) * unit) for u in range(0, units, per)]


def _order_token(v):
    return (v[0:2 * SUBLANES, 0:LANES] > 0).astype(F32) * 0.0


def _attention_scores(q2, keys, lmask):
    rows = [k for pair in keys for k in pair if k is not None]
    s = _dot_nt(q2, jnp.concatenate(rows, axis=0))
    scores, off = [], 0
    for kcm, klm in keys:
        s_c = s[:, off:off + kcm.shape[0]]
        off += kcm.shape[0]
        s_l = None
        if klm is not None:
            s_l = jnp.where(lmask, s[:, off:off + klm.shape[0]], NEG)
            off += klm.shape[0]
        scores.append((s_c, s_l))
    return scores


def _attention_output(scores, values, sinks):
    weights, rows, inv = [], [], []
    for (s_c, s_l), (vcm, vlm), sink in zip(scores, values, sinks):
        mx = jnp.maximum(jnp.max(s_c, axis=-1, keepdims=True), sink)
        if s_l is not None:
            mx = jnp.maximum(mx, jnp.max(s_l, axis=-1, keepdims=True))
        e_c = jnp.exp2(s_c - mx)
        den = jnp.exp2(sink - mx) + jnp.sum(e_c, axis=-1, keepdims=True)
        weights.append(e_c.astype(BF16))
        rows.append(vcm)
        if s_l is not None:
            e_l = jnp.exp2(s_l - mx)
            den = den + jnp.sum(e_l, axis=-1, keepdims=True)
            weights.append(e_l.astype(BF16))
            rows.append(vlm)
        inv.append(1.0 / den)
    out = _dot(jnp.concatenate(weights, axis=1), jnp.concatenate(rows, axis=0))
    low = lax.broadcasted_iota(jnp.int32, (1, LANES), 1) < HEAD_DIM
    return out * jnp.where(low, inv[0], inv[1])


def _mixer_kernel(*refs, n, tm, has_local, final_norm, attn_w, conv_w, pool_w, d_ff):
    it = iter(refs)
    x_ref, q_ref = next(it), next(it)
    if has_local:
        kk_ref, vv_ref = next(it), next(it)
    kc_ref, vc_ref = next(it), next(it)
    h_ref, p_ref, mod_ref, sink_ref = (next(it) for _ in range(4))
    dw_ref, cvec_ref, band_ref, poolw_ref, wout_ref, g2_ref, wfi_ref, wfo_ref, gf_ref = (next(it) for _ in range(9))
    out_ref = next(it)
    mix_ref, hp_ref, hs_ref, y_ref, y2_ref = (next(it) for _ in range(5))

    s = pl.program_id(0)
    tiles = pl.num_programs(0) - 1
    tiles_per_seq = n // tm
    i = lax.rem(jnp.minimum(s, tiles - 1), tiles_per_seq)
    last_i = tiles_per_seq - 1
    t0 = pl.multiple_of(i * tm, tm)

    @pl.when(s == 0)
    def _():
        rows = 2 * SUBLANES

        def zero_rows(r, carry):
            mix_ref[pl.ds(pl.multiple_of(r * rows, rows), rows), :] = jnp.zeros((rows, mix_ref.shape[1]), BF16)
            return carry

        lax.fori_loop(0, tm // rows, zero_rows, 0)

    ffn_state = {}

    def out_project():
        ffn_state["o"] = _dot(mix_ref[...], wout_ref[:, 0:x_ref.shape[2]])

    def ffn_norm():
        x1 = x_ref[0] + mod_ref[0, 2:3, :] * ffn_state["o"]
        y2 = _rms_norm(x1, g2_ref[...] * (1.0 + mod_ref[0, 4:5, :])) + mod_ref[0, 3:4, :]
        y2_ref[...] = y2.astype(BF16)
        ffn_state["x1"] = x1

    ffn_parts = []
    deferred = []
    tokens = []

    def ffn_stages(a, w):
        state = {}

        def hidden():
            if tokens:
                corner = (slice(0, 2 * SUBLANES), slice(0, LANES))
                y2_ref[corner] = y2_ref[corner] + sum(tokens).astype(BF16)
                tokens.clear()
            lhs = y2_ref[...]
            state["gate"] = _dot(lhs, wfi_ref[:, a:a + w])
            state["up"] = _dot(lhs, wfi_ref[:, d_ff + a:d_ff + a + w])

        def project():
            gate = state["gate"]
            act = (gate * jax.nn.sigmoid(gate) * state["up"]).astype(BF16)
            part = _dot(act, wfo_ref[a:a + w, 0:x_ref.shape[2]])
            ffn_parts[:] = [part if not ffn_parts else ffn_parts[0] + part]
        return hidden, project

    ffn_pieces = [ffn_stages(a, w) for a, w in _ffn_chunks(d_ff)]


    zero = jnp.zeros((), BF16)
    n_groups = kc_ref.shape[2] // (2 * LANES)
    heads_per_group = attn_w // HEAD_DIM // n_groups

    def attention_stages(blk, g, pr):
        state = {}
        rows = slice(blk * Q_BLOCK, (blk + 1) * Q_BLOCK)
        col = (g * heads_per_group // 2 + pr) * LANES
        head = g * heads_per_group + 2 * pr

        half_cols = [slice((2 * g + hh) * LANES, (2 * g + hh + 1) * LANES) for hh in range(2)]
        start = t0 + blk * Q_BLOCK
        s0 = pl.multiple_of(jnp.clip(start - WINDOW, 0, n - SPAN), Q_BLOCK) if has_local else None

        def scores():
            if has_local:
                r = lax.broadcasted_iota(jnp.int32, (Q_BLOCK, SPAN), 0)
                c = lax.broadcasted_iota(jnp.int32, (Q_BLOCK, SPAN), 1)
                dist = (start - s0) + r - c
                lmask = (dist >= -WINDOW) & (dist <= WINDOW)
            else:
                lmask = None
            keys = [(kc_ref[0, :, cs], kk_ref[0, pl.ds(s0, SPAN), cs] if has_local else None) for cs in half_cols]
            state["scores"] = _attention_scores(q_ref[0, rows, col:col + LANES], keys, lmask)

        def output():
            values = [(vc_ref[0, :, cs], vv_ref[0, pl.ds(s0, SPAN), cs] if has_local else None) for cs in half_cols]
            out = _attention_output(state["scores"], values, (sink_ref[head] * LOG2E, sink_ref[head + 1] * LOG2E))
            deferred.append((mix_ref, (rows, slice(col, col + LANES)), out.astype(BF16)))
        return scores, output

    attention_pieces = [attention_stages(blk, g, pr) for blk in range(tm // Q_BLOCK)
                        for g in range(n_groups) for pr in range(heads_per_group // 2)]

    dw_b, ln_g, ln_b, pool_scale = (cvec_ref[k:k + 1, :] for k in range(4))

    def conv_prepare():
        hp_ref[CONV_HALO:CONV_HALO + tm, :] = h_ref[0, pl.ds(t0, tm), :].astype(F32)
        prev = h_ref[0, pl.ds(pl.multiple_of(jnp.maximum(t0 - CONV_HALO, 0), CONV_HALO), CONV_HALO), :]
        hp_ref[0:CONV_HALO, :] = jnp.where(i > 0, prev.astype(F32), 0.0)
        nxt = h_ref[0, pl.ds(pl.multiple_of(jnp.minimum(t0 + tm, n - CONV_HALO), CONV_HALO), CONV_HALO), :]
        hp_ref[CONV_HALO + tm:, :] = jnp.where(i < last_i, nxt.astype(F32), 0.0)

    def conv_piece(r0):
        def run():
            for sh in range(1, SUBLANES):
                hs_ref[sh - 1] = hp_ref[r0 + sh:r0 + sh + hs_ref.shape[1], :]
            acc = jnp.zeros((ROW_CHUNK, conv_w), F32)
            for k in range(CONV_KERNEL):
                base = CONV_HALO - CONV_PAD + k
                sh = base % SUBLANES
                if sh == 0:
                    tap = hp_ref[r0 + base:r0 + base + ROW_CHUNK, :]
                else:
                    tap = hs_ref[sh - 1, base - sh:base - sh + ROW_CHUNK, :]
                acc = acc + tap * dw_ref[k:k + 1, :]
            hc = acc + dw_b
            mu = jnp.mean(hc, axis=-1, keepdims=True)
            cen = hc - mu
            var = jnp.mean(cen * cen, axis=-1, keepdims=True)
            hn = cen * lax.rsqrt(var + EPS) * ln_g + ln_b
            res = hn * jax.nn.sigmoid(hn)
            deferred.append((mix_ref, (slice(r0, r0 + ROW_CHUNK), slice(attn_w, attn_w + conv_w)), res.astype(BF16)))
            tokens.append(_order_token(res))
        return run

    group_w = pool_w // len(POOL_WINDOWS)

    def pool_block(blk):
        def run():
            start = t0 + blk * Q_BLOCK
            s0 = pl.multiple_of(jnp.clip(start - POOL_HALO, 0, n - POOL_SPAN), POOL_HALO)
            which = (start - s0) // POOL_HALO
            span = p_ref[0, pl.ds(s0, POOL_SPAN), :]
            lane = lax.broadcasted_iota(jnp.int32, (1, pool_w), 1)
            y = None
            for gi in range(len(POOL_WINDOWS)):
                in_group = (lane >= gi * group_w) & (lane < (gi + 1) * group_w)
                part = _dot(band_ref[which, gi], jnp.where(in_group, span, zero))
                y = part if y is None else y + part
            y_ref[blk * Q_BLOCK:(blk + 1) * Q_BLOCK, :] = y.astype(BF16)
        return run

    def pool_project():
        pooled = _dot(y_ref[...], poolw_ref[...]) * pool_scale
        deferred.append((mix_ref, (slice(None), slice(attn_w + conv_w, None)), pooled.astype(BF16)))

    n_chunks = len(ffn_pieces)
    per_group = -(-len(attention_pieces) // n_chunks)
    groups = [attention_pieces[j * per_group:(j + 1) * per_group] for j in range(n_chunks)]
    score_stage = lambda j: [scores for scores, _ in groups[j]]
    output_stage = lambda j: [output for _, output in groups[j]]
    hidden_stage = lambda j: [ffn_pieces[j][0]]
    project_stage = lambda j: [ffn_pieces[j][1]]
    elementwise = ([conv_prepare] + [conv_piece(r0) for r0 in range(0, tm, ROW_CHUNK)]
                   + [pool_block(blk) for blk in range(tm // Q_BLOCK)])
    per_elem = -(-len(elementwise) // n_chunks)
    order = [out_project, ffn_norm] + score_stage(0)
    for j in range(n_chunks):
        order += hidden_stage(j) + elementwise[j * per_elem:(j + 1) * per_elem] + output_stage(j)
        if j + 1 < n_chunks:
            order += score_stage(j + 1)
        order += project_stage(j)
    order += [pool_project]
    for stage in order:
        stage()

    x2 = ffn_state["x1"] + mod_ref[0, 5:6, :] * ffn_parts[0]
    if final_norm:
        x2 = _rms_norm(x2, gf_ref[...])
    out_ref[0] = x2
    for ref, index, value in deferred:
        ref[index] = value


def _mixer(x, q, kk, vv, kc, vc, h, p, mod, sink, dw, cvec, band, poolw, wout, g2, wfi, wfo, gf,
           *, tm, final_norm):
    b, n, d = x.shape
    has_local = kk is not None
    attn_w, kv_w, conv_w, pool_w = q.shape[2], kc.shape[2], h.shape[2], p.shape[2]
    n_ctx = kc.shape[1]
    d_ff = wfo.shape[0]
    tiles_per_seq = n // tm
    tiles = b * tiles_per_seq
    mix_tile = lambda s: jnp.minimum(s, tiles - 1)
    ffn_tile = lambda s: jnp.maximum(s - 1, 0)
    const = lambda shape: pl.BlockSpec(shape, lambda s: (0,) * len(shape), pipeline_mode=pl.Buffered(1))
    tok = lambda width, tile: pl.BlockSpec(
        (1, tm, width), lambda s: (tile(s) // tiles_per_seq, tile(s) % tiles_per_seq, 0))
    seq = lambda length, width, tile: pl.BlockSpec((1, length, width), lambda s: (tile(s) // tiles_per_seq, 0, 0))
    in_specs = [tok(d, ffn_tile), tok(attn_w, mix_tile)]
    args = [x, q]
    if has_local:
        in_specs += [seq(n, kv_w, mix_tile), seq(n, kv_w, mix_tile)]
        args += [kk, vv]
    in_specs += [seq(n_ctx, kv_w, mix_tile), seq(n_ctx, kv_w, mix_tile), seq(n, conv_w, mix_tile),
                 seq(n, pool_w, mix_tile), seq(6, d, ffn_tile),
                 pl.BlockSpec(memory_space=pltpu.SMEM),
                 const(dw.shape), const(cvec.shape), const(band.shape), const(poolw.shape), const(wout.shape),
                 const(g2.shape), const(wfi.shape), const(wfo.shape), const(gf.shape)]
    args += [kc, vc, h, p, mod, sink, dw, cvec, band, poolw, wout, g2, wfi, wfo, gf]
    kern = functools.partial(_mixer_kernel, n=n, tm=tm, has_local=has_local, final_norm=final_norm,
                             attn_w=attn_w, conv_w=conv_w, pool_w=pool_w, d_ff=d_ff)
    return pl.pallas_call(
        kern, grid=(tiles + 1,), in_specs=in_specs, out_specs=tok(d, ffn_tile),
        out_shape=jax.ShapeDtypeStruct((b, n, d), F32),
        scratch_shapes=[
            pltpu.VMEM((tm, attn_w + conv_w + pool_w), BF16),
            pltpu.VMEM((tm + 2 * CONV_HALO, conv_w), F32),
            pltpu.VMEM((SUBLANES - 1, ROW_CHUNK + 2 * CONV_HALO - SUBLANES, conv_w), F32),
            pltpu.VMEM((tm, pool_w), BF16),
            pltpu.VMEM((tm, d), BF16),
        ],
        compiler_params=pltpu.CompilerParams(
            dimension_semantics=("arbitrary",), vmem_limit_bytes=VMEM_LIMIT_BYTES),
        name="mixer_local" if has_local else "mixer_ctx",
    )(*args)


def _rope_tables(n):
    rows = n // GRID_W
    row = jnp.repeat(jnp.arange(rows), GRID_W).astype(F32)
    col = jnp.tile(jnp.arange(GRID_W), rows).astype(F32)
    half = HEAD_DIM // 2
    inv = ROPE_BASE ** (-jnp.arange(0, half, 2, dtype=F32) / half)
    ar = row[:, None] * inv
    ac = col[:, None] * inv
    ang = jnp.concatenate([ar, ar, ac, ac], axis=-1)
    cos, sin = jnp.cos(ang), jnp.sin(ang)
    first = (jnp.arange(HEAD_DIM) // (HEAD_DIM // 4)) % 2 == 0
    sin_a = jnp.where(first, -sin, 0.0)
    sin_b = jnp.where(first, 0.0, sin)
    rep = LANES // HEAD_DIM
    return tuple(jnp.tile(t, (1, rep)) for t in (cos, sin_a, sin_b))


def _pool_bands(n):
    assert n >= POOL_SPAN and n % Q_BLOCK == 0
    r = jnp.arange(Q_BLOCK)[:, None]
    pos = jnp.arange(POOL_SPAN)[None, :]
    variants = []
    for k in range(3):
        t = k * POOL_HALO + r
        first = 0 if k == 0 else -POOL_SPAN
        last = POOL_SPAN - 1 if k == 2 else 2 * POOL_SPAN
        per_window = []
        for win in POOL_WINDOWS:
            lo = jnp.maximum(t - win // 2, first)
            hi = jnp.minimum(t + win - 1 - win // 2, last)
            inside = (pos >= lo) & (pos <= hi)
            mean = jnp.where(inside, 1.0 / (hi - lo + 1).astype(F32), 0.0)
            per_window.append(mean - (pos == t).astype(F32))
        variants.append(jnp.stack(per_window))
    return jnp.stack(variants).astype(BF16)


def _pad_lanes(w):
    return jnp.pad(w, ((0, 0), (0, LANES)))


def kernel(x, c, ctx, c_ctx, w_mod, b_mod, norm1_g, norm2_g, w_in, conv_dw, conv_dw_b, conv_ln_g, conv_ln_b,
           attn_sink, pool_w, pool_scale, w_out, w_ffn_in, w_ffn_out, final_g):
    b, n, d = x.shape
    n_ctx = ctx.shape[1]
    depth = w_mod.shape[0]
    heads = attn_sink.shape[1]
    attn_w = heads * HEAD_DIM
    conv_w = conv_dw.shape[2]
    pool_wd = pool_scale.shape[1]
    kv_w = w_in.shape[2] - attn_w - 2 * conv_w - pool_wd
    kv_w //= 2
    assert kv_w % LANES == 0
    dims = (attn_w, kv_w, conv_w, pool_wd)

    rows = -(-(b + 1) // SUBLANES) * SUBLANES
    cc = jnp.zeros((rows, d), F32).at[:b].set(c).at[b].set(c_ctx)
    mod_all = _modulation(cc, w_mod, b_mod)

    tables = _rope_tables(n)
    bands, bands_ctx = _pool_bands(n), _pool_bands(n_ctx)
    tm_in = min(1024, n)
    tm_mix = min(512, n)
    ctx_rows = tm_in if (b * n_ctx) % tm_in == 0 and tm_in % n_ctx == 0 else n_ctx
    ctx_groups = b * n_ctx // ctx_rows
    cx = ctx
    for l in range(depth):
        last = l == depth - 1
        mod = mod_all[l, :b].reshape(b, 6, d)
        mod_c = jnp.broadcast_to(mod_all[l, b].reshape(1, 6, d), (b, 6, d))
        w_all = w_in[l].astype(BF16)
        w_kv = w_all[:, attn_w:attn_w + 2 * kv_w]
        g1 = norm1_g[l].reshape(1, d)
        g2 = norm2_g[l].reshape(1, d)
        cvec = jnp.zeros((SUBLANES, conv_w), F32).at[0].set(conv_dw_b[l]).at[1].set(conv_ln_g[l]) \
            .at[2].set(conv_ln_b[l]).at[3].set(pool_scale[l])
        poolw = jax.scipy.linalg.block_diag(*[pool_w[l, gi] for gi in range(pool_w.shape[1])]).astype(BF16)
        head = (attn_sink[l], conv_dw[l], cvec)
        shared = (poolw, _pad_lanes(w_out[l].astype(BF16)), g2,
                  w_ffn_in[l].astype(BF16), _pad_lanes(w_ffn_out[l].astype(BF16)), final_g.reshape(1, d))

        q, kk, vv, h, p = _inproj(x, mod, g1, w_all, tables, tm=tm_in, kv_only=False, dims=dims)
        ctx_outs = _inproj(cx.reshape(ctx_groups, ctx_rows, d), mod_c[:ctx_groups], g1, w_kv if last else w_all,
                           None, tm=ctx_rows, kv_only=last, dims=dims)
        ctx_outs = [t.reshape(b, n_ctx, t.shape[-1]) for t in ctx_outs]
        if last:
            kc, vc = ctx_outs
        else:
            qc, kc, vc, hc, pc = ctx_outs
        x = _mixer(x, q, kk, vv, kc, vc, h, p, mod, *head, bands, *shared, tm=tm_mix, final_norm=last)
        if not last:
            cx = _mixer(cx, qc, None, None, kc, vc, hc, pc, mod_c, *head, bands_ctx, *shared,
                        tm=n_ctx, final_norm=False)
    return x
```

```python
import functools

import jax
import jax.numpy as jnp
from jax import lax
from jax.experimental import pallas as pl
from jax.experimental.pallas import tpu as pltpu

F32 = jnp.float32
BF16 = jnp.bfloat16

LANES = 128
SUBLANES = 8
MXU_COLS = 256
FFN_CHUNK_TILES = 3
VMEM_LIMIT_BYTES = 62 * 1024 * 1024

GRID_W = 64
HEAD_DIM = 64
ROPE_BASE = 10000.0
WINDOW = 128
Q_BLOCK = 128
SPAN = Q_BLOCK + 2 * WINDOW
CONV_KERNEL = 31
CONV_PAD = CONV_KERNEL // 2
CONV_HALO = 16
POOL_WINDOWS = (2, 4, 8, 16)
POOL_HALO = 16
POOL_SPAN = Q_BLOCK + 2 * POOL_HALO
ROW_CHUNK = 64
INPROJ_ROW_GROUPS = 1
EPS = 1e-6
NEG = -1e30
LOG2E = 1.4426950408889634


def _rms_norm(x, g):
    return x * lax.rsqrt(jnp.mean(x * x, axis=-1, keepdims=True) + EPS) * g


_dot = functools.partial(jnp.dot, preferred_element_type=F32)
_dot_nt = functools.partial(lax.dot_general, dimension_numbers=(((1,), (1,)), ((), ())), preferred_element_type=F32)


def _mod_kernel(c_ref, w_ref, b_ref, o_ref):
    c = c_ref[...]
    a = c * jax.nn.sigmoid(c)
    o_ref[0] = jnp.dot(a, w_ref[0], preferred_element_type=F32,
                       precision=lax.Precision.HIGHEST) + b_ref[0]


def _modulation(cc, w_mod, b_mod):
    depth, d, d6 = w_mod.shape
    rows = cc.shape[0]
    tn = d6 // 4
    return pl.pallas_call(
        _mod_kernel,
        grid=(depth, d6 // tn),
        in_specs=[
            pl.BlockSpec((rows, d), lambda l, j: (0, 0)),
            pl.BlockSpec((1, d, tn), lambda l, j: (l, 0, j)),
            pl.BlockSpec((1, 1, tn), lambda l, j: (l, 0, j)),
        ],
        out_specs=pl.BlockSpec((1, rows, tn), lambda l, j: (l, 0, j)),
        out_shape=jax.ShapeDtypeStruct((depth, rows, d6), F32),
        compiler_params=pltpu.CompilerParams(
            dimension_semantics=("arbitrary", "arbitrary"), vmem_limit_bytes=VMEM_LIMIT_BYTES),
        name="modulation",
    )(cc, w_mod, b_mod.reshape(depth, 1, d6))


def _rope(t, cos, sin_a, sin_b):
    quarter = HEAD_DIM // 4
    return (t * cos + pltpu.roll(t, LANES - quarter, 1) * sin_a + pltpu.roll(t, quarter, 1) * sin_b)


def _inproj_kernel(*refs, rope, kv_only, attn_w, kv_w, conv_w):
    if rope:
        x_ref, mod_ref, g_ref, w_ref, cos_ref, sa_ref, sb_ref = refs[:7]
        outs = refs[7:]
    else:
        x_ref, mod_ref, g_ref, w_ref = refs[:4]
        outs = refs[4:]
    tm = x_ref.shape[1]
    groups = [slice(r, r + tm // INPROJ_ROW_GROUPS) for r in range(0, tm, tm // INPROJ_ROW_GROUPS)]
    gain = g_ref[...] * (1.0 + mod_ref[0, 1:2, :])
    projected = []
    for rs in groups:
        hl = _rms_norm(x_ref[0, rs, :], gain) + mod_ref[0, 0:1, :]
        projected.append(_dot(hl.astype(BF16), w_ref[...]))
    low = lax.broadcasted_iota(jnp.int32, (1, LANES), 1) < HEAD_DIM
    high = jnp.logical_not(low)
    for rs, u in zip(groups, projected):
        def maybe_rope(t):
            if rope:
                return _rope(t, cos_ref[rs, :], sa_ref[rs, :], sb_ref[rs, :])
            return t

        if kv_only:
            kk_ref, vv_ref = outs
            off = 0
        else:
            q_ref, kk_ref, vv_ref, h_ref, p_ref = outs
            for c in range(attn_w // LANES):
                t = maybe_rope(u[:, c * LANES:(c + 1) * LANES])
                q_ref[0, rs, c * LANES:(c + 1) * LANES] = (t * (HEAD_DIM ** -0.5 * LOG2E)).astype(BF16)
            off = attn_w
        for c in range(kv_w // LANES):
            k = maybe_rope(u[:, off + c * LANES: off + (c + 1) * LANES])
            v = u[:, off + kv_w + c * LANES: off + kv_w + (c + 1) * LANES]
            for t, t_ref in ((k, kk_ref), (v, vv_ref)):
                swapped = pltpu.roll(t, HEAD_DIM, 1)
                for j, (keep, src) in enumerate(((low, t), (high, swapped), (low, swapped), (high, t))):
                    cols = slice((4 * c + j) * LANES, (4 * c + j + 1) * LANES)
                    t_ref[0, rs, cols] = jnp.where(keep, src, 0.0).astype(BF16)
        off += 2 * kv_w
        if not kv_only:
            a = u[:, off:off + conv_w]
            g = u[:, off + conv_w:off + 2 * conv_w]
            h_ref[0, rs, :] = (a * jax.nn.sigmoid(g)).astype(BF16)
            off += 2 * conv_w
            p_ref[0, rs, :] = u[:, off:].astype(BF16)


def _inproj(x, mod, g, w, tables, *, tm, kv_only, dims):
    b, n, d = x.shape
    attn_w, kv_w, conv_w, pool_w = dims
    rope = tables is not None
    grid = (n // tm, b)
    in_specs = [
        pl.BlockSpec((1, tm, d), lambda i, bb: (bb, i, 0)),
        pl.BlockSpec((1, 6, d), lambda i, bb: (bb, 0, 0)),
        pl.BlockSpec((1, d), lambda i, bb: (0, 0)),
        pl.BlockSpec(w.shape, lambda i, bb: (0, 0)),
    ]
    args = [x, mod, g, w]
    if rope:
        in_specs += [pl.BlockSpec((tm, LANES), lambda i, bb: (i, 0))] * 3
        args += list(tables)
    tok = lambda width: pl.BlockSpec((1, tm, width), lambda i, bb: (bb, i, 0))
    out_specs = [tok(4 * kv_w), tok(4 * kv_w)]
    out_shape = [jax.ShapeDtypeStruct((b, n, 4 * kv_w), BF16)] * 2
    if not kv_only:
        out_specs = [tok(attn_w)] + out_specs + [tok(conv_w), tok(pool_w)]
        out_shape = ([jax.ShapeDtypeStruct((b, n, attn_w), BF16)] + out_shape
                     + [jax.ShapeDtypeStruct((b, n, conv_w), BF16), jax.ShapeDtypeStruct((b, n, pool_w), BF16)])
    kern = functools.partial(_inproj_kernel, rope=rope, kv_only=kv_only,
                             attn_w=attn_w, kv_w=kv_w, conv_w=conv_w)
    return pl.pallas_call(
        kern, grid=grid, in_specs=in_specs, out_specs=out_specs, out_shape=out_shape,
        compiler_params=pltpu.CompilerParams(
            dimension_semantics=("arbitrary", "arbitrary"), vmem_limit_bytes=VMEM_LIMIT_BYTES),
        name="inproj_kv" if kv_only else "inproj",
    )(*args)


def _ffn_tile(d_ff):
    return MXU_COLS if d_ff % MXU_COLS == 0 else LANES


def _ffn_chunks(d_ff):
    unit = _ffn_tile(d_ff)
    units = d_ff // unit
    per = FFN_CHUNK_TILES
    return [(u * unit, min(per, units - u) * unit) for u in range(0, units, per)]


def _order_token(v):
    return (v[0:2 * SUBLANES, 0:LANES] > 0).astype(F32) * 0.0


def _attention_scores(q2, keys, lmask):
    rows = [k for pair in keys for k in pair if k is not None]
    s = _dot_nt(q2, jnp.concatenate(rows, axis=0))
    scores, off = [], 0
    for kcm, klm in keys:
        s_c = s[:, off:off + kcm.shape[0]]
        off += kcm.shape[0]
        s_l = None
        if klm is not None:
            s_l = jnp.where(lmask, s[:, off:off + klm.shape[0]], NEG)
            off += klm.shape[0]
        scores.append((s_c, s_l))
    return scores


def _attention_output(scores, values, sinks):
    weights, rows, inv = [], [], []
    for (s_c, s_l), (vcm, vlm), sink in zip(scores, values, sinks):
        mx = jnp.maximum(jnp.max(s_c, axis=-1, keepdims=True), sink)
        if s_l is not None:
            mx = jnp.maximum(mx, jnp.max(s_l, axis=-1, keepdims=True))
        e_c = jnp.exp2(s_c - mx)
        den = jnp.exp2(sink - mx) + jnp.sum(e_c, axis=-1, keepdims=True)
        weights.append(e_c.astype(BF16))
        rows.append(vcm)
        if s_l is not None:
            e_l = jnp.exp2(s_l - mx)
            den = den + jnp.sum(e_l, axis=-1, keepdims=True)
            weights.append(e_l.astype(BF16))
            rows.append(vlm)
        inv.append(1.0 / den)
    out = _dot(jnp.concatenate(weights, axis=1), jnp.concatenate(rows, axis=0))
    low = lax.broadcasted_iota(jnp.int32, (1, LANES), 1) < HEAD_DIM
    return out * jnp.where(low, inv[0], inv[1])


def _mixer_kernel(*refs, n, tm, has_local, final_norm, attn_w, conv_w, pool_w, d_ff):
    it = iter(refs)
    x_ref, q_ref = next(it), next(it)
    if has_local:
        kk_ref, vv_ref = next(it), next(it)
    kc_ref, vc_ref = next(it), next(it)
    h_ref, p_ref, mod_ref, sink_ref = (next(it) for _ in range(4))
    dw_ref, cvec_ref, band_ref, poolw_ref, wout_ref, g2_ref, wfi_ref, wfo_ref, gf_ref = (next(it) for _ in range(9))
    out_ref = next(it)
    mix_ref, hp_ref, hs_ref, y_ref, y2_ref = (next(it) for _ in range(5))

    s = pl.program_id(0)
    tiles = pl.num_programs(0) - 1
    tiles_per_seq = n // tm
    i = lax.rem(jnp.minimum(s, tiles - 1), tiles_per_seq)
    last_i = tiles_per_seq - 1
    t0 = pl.multiple_of(i * tm, tm)

    @pl.when(s == 0)
    def _():
        mix_ref[...] = jnp.zeros_like(mix_ref)

    ffn_state = {}

    def out_project():
        ffn_state["o"] = _dot(mix_ref[...], wout_ref[:, 0:x_ref.shape[2]])

    def ffn_norm():
        x1 = x_ref[0] + mod_ref[0, 2:3, :] * ffn_state["o"]
        y2 = _rms_norm(x1, g2_ref[...] * (1.0 + mod_ref[0, 4:5, :])) + mod_ref[0, 3:4, :]
        y2_ref[...] = y2.astype(BF16)
        ffn_state["x1"] = x1

    ffn_parts = []
    deferred = []
    tokens = []

    def ffn_stages(a, w):
        state = {}

        def hidden():
            if tokens:
                corner = (slice(0, 2 * SUBLANES), slice(0, LANES))
                y2_ref[corner] = y2_ref[corner] + sum(tokens).astype(BF16)
                tokens.clear()
            lhs = y2_ref[...]
            state["gate"] = _dot(lhs, wfi_ref[:, a:a + w])
            state["up"] = _dot(lhs, wfi_ref[:, d_ff + a:d_ff + a + w])

        def project():
            gate = state["gate"]
            act = (gate * jax.nn.sigmoid(gate) * state["up"]).astype(BF16)
            part = _dot(act, wfo_ref[a:a + w, 0:x_ref.shape[2]])
            ffn_parts[:] = [part if not ffn_parts else ffn_parts[0] + part]
        return hidden, project

    ffn_pieces = [ffn_stages(a, w) for a, w in _ffn_chunks(d_ff)]


    zero = jnp.zeros((), BF16)
    n_groups = kc_ref.shape[2] // (2 * LANES)
    heads_per_group = attn_w // HEAD_DIM // n_groups

    def attention_stages(blk, g, pr):
        state = {}
        rows = slice(blk * Q_BLOCK, (blk + 1) * Q_BLOCK)
        col = (g * heads_per_group // 2 + pr) * LANES
        head = g * heads_per_group + 2 * pr

        half_cols = [slice((2 * g + hh) * LANES, (2 * g + hh + 1) * LANES) for hh in range(2)]
        start = t0 + blk * Q_BLOCK
        s0 = pl.multiple_of(jnp.clip(start - WINDOW, 0, n - SPAN), Q_BLOCK) if has_local else None

        def scores():
            if has_local:
                r = lax.broadcasted_iota(jnp.int32, (Q_BLOCK, SPAN), 0)
                c = lax.broadcasted_iota(jnp.int32, (Q_BLOCK, SPAN), 1)
                dist = (start - s0) + r - c
                lmask = (dist >= -WINDOW) & (dist <= WINDOW)
            else:
                lmask = None
            keys = [(kc_ref[0, :, cs], kk_ref[0, pl.ds(s0, SPAN), cs] if has_local else None) for cs in half_cols]
            state["scores"] = _attention_scores(q_ref[0, rows, col:col + LANES], keys, lmask)

        def output():
            values = [(vc_ref[0, :, cs], vv_ref[0, pl.ds(s0, SPAN), cs] if has_local else None) for cs in half_cols]
            out = _attention_output(state["scores"], values, (sink_ref[head] * LOG2E, sink_ref[head + 1] * LOG2E))
            deferred.append((mix_ref, (rows, slice(col, col + LANES)), out.astype(BF16)))
        return scores, output

    attention_pieces = [attention_stages(blk, g, pr) for blk in range(tm // Q_BLOCK)
                        for g in range(n_groups) for pr in range(heads_per_group // 2)]

    dw_b, ln_g, ln_b, pool_scale = (cvec_ref[k:k + 1, :] for k in range(4))

    def conv_prepare():
        hp_ref[CONV_HALO:CONV_HALO + tm, :] = h_ref[0, pl.ds(t0, tm), :].astype(F32)
        prev = h_ref[0, pl.ds(pl.multiple_of(jnp.maximum(t0 - CONV_HALO, 0), CONV_HALO), CONV_HALO), :]
        hp_ref[0:CONV_HALO, :] = jnp.where(i > 0, prev.astype(F32), 0.0)
        nxt = h_ref[0, pl.ds(pl.multiple_of(jnp.minimum(t0 + tm, n - CONV_HALO), CONV_HALO), CONV_HALO), :]
        hp_ref[CONV_HALO + tm:, :] = jnp.where(i < last_i, nxt.astype(F32), 0.0)

    def conv_piece(r0):
        def run():
            for sh in range(1, SUBLANES):
                hs_ref[sh - 1] = hp_ref[r0 + sh:r0 + sh + hs_ref.shape[1], :]
            acc = jnp.zeros((ROW_CHUNK, conv_w), F32)
            for k in range(CONV_KERNEL):
                base = CONV_HALO - CONV_PAD + k
                sh = base % SUBLANES
                if sh == 0:
                    tap = hp_ref[r0 + base:r0 + base + ROW_CHUNK, :]
                else:
                    tap = hs_ref[sh - 1, base - sh:base - sh + ROW_CHUNK, :]
                acc = acc + tap * dw_ref[k:k + 1, :]
            hc = acc + dw_b
            mu = jnp.mean(hc, axis=-1, keepdims=True)
            cen = hc - mu
            var = jnp.mean(cen * cen, axis=-1, keepdims=True)
            hn = cen * lax.rsqrt(var + EPS) * ln_g + ln_b
            res = hn * jax.nn.sigmoid(hn)
            deferred.append((mix_ref, (slice(r0, r0 + ROW_CHUNK), slice(attn_w, attn_w + conv_w)), res.astype(BF16)))
            tokens.append(_order_token(res))
        return run

    group_w = pool_w // len(POOL_WINDOWS)

    def pool_block(blk):
        def run():
            start = t0 + blk * Q_BLOCK
            s0 = pl.multiple_of(jnp.clip(start - POOL_HALO, 0, n - POOL_SPAN), POOL_HALO)
            which = (start - s0) // POOL_HALO
            span = p_ref[0, pl.ds(s0, POOL_SPAN), :]
            lane = lax.broadcasted_iota(jnp.int32, (1, pool_w), 1)
            y = None
            for gi in range(len(POOL_WINDOWS)):
                in_group = (lane >= gi * group_w) & (lane < (gi + 1) * group_w)
                part = _dot(band_ref[which, gi], jnp.where(in_group, span, zero))
                y = part if y is None else y + part
            y_ref[blk * Q_BLOCK:(blk + 1) * Q_BLOCK, :] = y.astype(BF16)
        return run

    def pool_project():
        pooled = _dot(y_ref[...], poolw_ref[...]) * pool_scale
        deferred.append((mix_ref, (slice(None), slice(attn_w + conv_w, None)), pooled.astype(BF16)))

    n_chunks = len(ffn_pieces)
    per_group = -(-len(attention_pieces) // n_chunks)
    groups = [attention_pieces[j * per_group:(j + 1) * per_group] for j in range(n_chunks)]
    score_stage = lambda j: [scores for scores, _ in groups[j]]
    output_stage = lambda j: [output for _, output in groups[j]]
    hidden_stage = lambda j: [ffn_pieces[j][0]]
    project_stage = lambda j: [ffn_pieces[j][1]]
    pool_blocks = [pool_block(blk) for blk in range(tm // Q_BLOCK)]
    elementwise = [conv_prepare] + [conv_piece(r0) for r0 in range(0, tm, ROW_CHUNK)]
    early = 1 if has_local else min(2, n_chunks)
    if has_local:
        elementwise += pool_blocks
    per_elem = -(-len(elementwise) // n_chunks)
    order = [out_project, ffn_norm]
    for j in range(early):
        order += score_stage(j)
    if not has_local:
        order += pool_blocks
    for j in range(n_chunks):
        order += hidden_stage(j) + elementwise[j * per_elem:(j + 1) * per_elem] + output_stage(j)
        if early <= j + 1 < n_chunks:
            order += score_stage(j + 1)
        order += project_stage(j)
    order += [pool_project]
    for stage in order:
        stage()

    x2 = ffn_state["x1"] + mod_ref[0, 5:6, :] * ffn_parts[0]
    if final_norm:
        x2 = _rms_norm(x2, gf_ref[...])
    out_ref[0] = x2
    for ref, index, value in deferred:
        ref[index] = value


def _mixer(x, q, kk, vv, kc, vc, h, p, mod, sink, dw, cvec, band, poolw, wout, g2, wfi, wfo, gf,
           *, tm, final_norm):
    b, n, d = x.shape
    has_local = kk is not None
    attn_w, kv_w, conv_w, pool_w = q.shape[2], kc.shape[2], h.shape[2], p.shape[2]
    n_ctx = kc.shape[1]
    d_ff = wfo.shape[0]
    tiles_per_seq = n // tm
    tiles = b * tiles_per_seq
    mix_tile = lambda s: jnp.minimum(s, tiles - 1)
    ffn_tile = lambda s: jnp.maximum(s - 1, 0)
    const = lambda shape: pl.BlockSpec(shape, lambda s: (0,) * len(shape), pipeline_mode=pl.Buffered(1))
    tok = lambda width, tile: pl.BlockSpec(
        (1, tm, width), lambda s: (tile(s) // tiles_per_seq, tile(s) % tiles_per_seq, 0))
    seq = lambda length, width, tile: pl.BlockSpec((1, length, width), lambda s: (tile(s) // tiles_per_seq, 0, 0))
    in_specs = [tok(d, ffn_tile), tok(attn_w, mix_tile)]
    args = [x, q]
    if has_local:
        in_specs += [seq(n, kv_w, mix_tile), seq(n, kv_w, mix_tile)]
        args += [kk, vv]
    in_specs += [seq(n_ctx, kv_w, mix_tile), seq(n_ctx, kv_w, mix_tile), seq(n, conv_w, mix_tile),
                 seq(n, pool_w, mix_tile), seq(6, d, ffn_tile),
                 pl.BlockSpec(memory_space=pltpu.SMEM),
                 const(dw.shape), const(cvec.shape), const(band.shape), const(poolw.shape), const(wout.shape),
                 const(g2.shape), const(wfi.shape), const(wfo.shape), const(gf.shape)]
    args += [kc, vc, h, p, mod, sink, dw, cvec, band, poolw, wout, g2, wfi, wfo, gf]
    kern = functools.partial(_mixer_kernel, n=n, tm=tm, has_local=has_local, final_norm=final_norm,
                             attn_w=attn_w, conv_w=conv_w, pool_w=pool_w, d_ff=d_ff)
    return pl.pallas_call(
        kern, grid=(tiles + 1,), in_specs=in_specs, out_specs=tok(d, ffn_tile),
        out_shape=jax.ShapeDtypeStruct((b, n, d), F32),
        scratch_shapes=[
            pltpu.VMEM((tm, attn_w + conv_w + pool_w), BF16),
            pltpu.VMEM((tm + 2 * CONV_HALO, conv_w), F32),
            pltpu.VMEM((SUBLANES - 1, ROW_CHUNK + 2 * CONV_HALO - SUBLANES, conv_w), F32),
            pltpu.VMEM((tm, pool_w), BF16),
            pltpu.VMEM((tm, d), BF16),
        ],
        compiler_params=pltpu.CompilerParams(
            dimension_semantics=("arbitrary",), vmem_limit_bytes=VMEM_LIMIT_BYTES),
        name="mixer_local" if has_local else "mixer_ctx",
    )(*args)


def _rope_tables(n):
    rows = n // GRID_W
    row = jnp.repeat(jnp.arange(rows), GRID_W).astype(F32)
    col = jnp.tile(jnp.arange(GRID_W), rows).astype(F32)
    half = HEAD_DIM // 2
    inv = ROPE_BASE ** (-jnp.arange(0, half, 2, dtype=F32) / half)
    ar = row[:, None] * inv
    ac = col[:, None] * inv
    ang = jnp.concatenate([ar, ar, ac, ac], axis=-1)
    cos, sin = jnp.cos(ang), jnp.sin(ang)
    first = (jnp.arange(HEAD_DIM) // (HEAD_DIM // 4)) % 2 == 0
    sin_a = jnp.where(first, -sin, 0.0)
    sin_b = jnp.where(first, 0.0, sin)
    rep = LANES // HEAD_DIM
    return tuple(jnp.tile(t, (1, rep)) for t in (cos, sin_a, sin_b))


def _pool_bands(n):
    assert n >= POOL_SPAN and n % Q_BLOCK == 0
    r = jnp.arange(Q_BLOCK)[:, None]
    pos = jnp.arange(POOL_SPAN)[None, :]
    variants = []
    for k in range(3):
        t = k * POOL_HALO + r
        first = 0 if k == 0 else -POOL_SPAN
        last = POOL_SPAN - 1 if k == 2 else 2 * POOL_SPAN
        per_window = []
        for win in POOL_WINDOWS:
            lo = jnp.maximum(t - win // 2, first)
            hi = jnp.minimum(t + win - 1 - win // 2, last)
            inside = (pos >= lo) & (pos <= hi)
            mean = jnp.where(inside, 1.0 / (hi - lo + 1).astype(F32), 0.0)
            per_window.append(mean - (pos == t).astype(F32))
        variants.append(jnp.stack(per_window))
    return jnp.stack(variants).astype(BF16)


def _pad_lanes(w):
    return jnp.pad(w, ((0, 0), (0, LANES)))


def kernel(x, c, ctx, c_ctx, w_mod, b_mod, norm1_g, norm2_g, w_in, conv_dw, conv_dw_b, conv_ln_g, conv_ln_b,
           attn_sink, pool_w, pool_scale, w_out, w_ffn_in, w_ffn_out, final_g):
    b, n, d = x.shape
    n_ctx = ctx.shape[1]
    depth = w_mod.shape[0]
    heads = attn_sink.shape[1]
    attn_w = heads * HEAD_DIM
    conv_w = conv_dw.shape[2]
    pool_wd = pool_scale.shape[1]
    kv_w = w_in.shape[2] - attn_w - 2 * conv_w - pool_wd
    kv_w //= 2
    assert kv_w % LANES == 0
    dims = (attn_w, kv_w, conv_w, pool_wd)

    rows = -(-(b + 1) // SUBLANES) * SUBLANES
    cc = jnp.zeros((rows, d), F32).at[:b].set(c).at[b].set(c_ctx)
    mod_all = _modulation(cc, w_mod, b_mod)

    tables = _rope_tables(n)
    bands, bands_ctx = _pool_bands(n), _pool_bands(n_ctx)
    tm_in = min(1024, n)
    tm_mix = min(512, n)
    ctx_rows = tm_in if (b * n_ctx) % tm_in == 0 and tm_in % n_ctx == 0 else n_ctx
    ctx_groups = b * n_ctx // ctx_rows
    cx = ctx
    for l in range(depth):
        last = l == depth - 1
        mod = mod_all[l, :b].reshape(b, 6, d)
        mod_c = jnp.broadcast_to(mod_all[l, b].reshape(1, 6, d), (b, 6, d))
        w_all = w_in[l].astype(BF16)
        w_kv = w_all[:, attn_w:attn_w + 2 * kv_w]
        g1 = norm1_g[l].reshape(1, d)
        g2 = norm2_g[l].reshape(1, d)
        cvec = jnp.zeros((SUBLANES, conv_w), F32).at[0].set(conv_dw_b[l]).at[1].set(conv_ln_g[l]) \
            .at[2].set(conv_ln_b[l]).at[3].set(pool_scale[l])
        poolw = jax.scipy.linalg.block_diag(*[pool_w[l, gi] for gi in range(pool_w.shape[1])]).astype(BF16)
        head = (attn_sink[l], conv_dw[l], cvec)
        shared = (poolw, _pad_lanes(w_out[l].astype(BF16)), g2,
                  w_ffn_in[l].astype(BF16), _pad_lanes(w_ffn_out[l].astype(BF16)), final_g.reshape(1, d))

        q, kk, vv, h, p = _inproj(x, mod, g1, w_all, tables, tm=tm_in, kv_only=False, dims=dims)
        ctx_outs = _inproj(cx.reshape(ctx_groups, ctx_rows, d), mod_c[:ctx_groups], g1, w_kv if last else w_all,
                           None, tm=ctx_rows, kv_only=last, dims=dims)
        ctx_outs = [t.reshape(b, n_ctx, t.shape[-1]) for t in ctx_outs]
        if last:
            kc, vc = ctx_outs
        else:
            qc, kc, vc, hc, pc = ctx_outs
        x = _mixer(x, q, kk, vv, kc, vc, h, p, mod, *head, bands, *shared, tm=tm_mix, final_norm=last)
        if not last:
            cx = _mixer(cx, qc, None, None, kc, vc, hc, pc, mod_c, *head, bands_ctx, *shared,
                        tm=n_ctx, final_norm=False)
    return x
```
